```python
import math
import jax
import jax.numpy as jnp
from jax import lax
import numpy as np


D_MODEL = 1024
BATCH = 4
SEQ = 8192
DEPTH = 4

GRID_W = 64
CTX_LEN = 256

N_MIXERS = 2
D_FF = 4 * D_MODEL
N_MOD = 6
NORM_EPS = 1e-6

MLA_HEADS = 8
MLA_NOPE = 128
MLA_ROPE = 64
MLA_V = 128
MLA_Q_RANK = 384
MLA_KV_RANK = 256
ROPE_BASE = 10000.0
Q_BLOCK = 128

GLA_HEADS = 4
GLA_DK = D_MODEL // 2 // GLA_HEADS
GLA_DV = D_MODEL // GLA_HEADS
GLA_GATE_RANK = 16
GLA_TAU = 16.0
GLA_CHUNK = 64

kernel_name = 'hybrid_mla_gla_diffusion_trunk'


def rms_norm(x, g):
    xf = x.astype(jnp.float32)
    y = xf * lax.rsqrt(jnp.mean(xf * xf, axis=-1, keepdims=True) + NORM_EPS)
    return (y * g.astype(jnp.float32)).astype(x.dtype)


def modulate(h, shift, scale):
    return h * (1.0 + scale) + shift


def sqrelu_mlp(h, w1, w2):
    return jnp.square(jax.nn.relu(h @ w1)) @ w2


def axial_rope_tables(n_tokens):
    rows = n_tokens // GRID_W
    row = jnp.repeat(jnp.arange(rows, dtype=jnp.float32), GRID_W)
    col = jnp.tile(jnp.arange(GRID_W, dtype=jnp.float32), rows)
    half = MLA_ROPE // 2
    inv_freq = ROPE_BASE ** (-jnp.arange(0, half, 2, dtype=jnp.float32) / half)
    ang_r = row[:, None] * inv_freq
    ang_c = col[:, None] * inv_freq
    return (jnp.cos(ang_r), jnp.sin(ang_r), jnp.cos(ang_c), jnp.sin(ang_c))


def _rotate_half(x, cos, sin):
    x1, x2 = jnp.split(x, 2, axis=-1)
    return jnp.concatenate([x1 * cos - x2 * sin, x1 * sin + x2 * cos], axis=-1)


def apply_axial_rope(x, rope):
    cos_r, sin_r, cos_c, sin_c = rope
    if x.ndim == 4:
        cos_r, sin_r, cos_c, sin_c = (t[:, None, :] for t in rope)
    x_r, x_c = jnp.split(x, 2, axis=-1)
    out = jnp.concatenate([_rotate_half(x_r, cos_r, sin_r), _rotate_half(x_c, cos_c, sin_c)], axis=-1)
    return out.astype(x.dtype)


def mla_attend(q_nope, q_rope, k_nope, k_rope, v):
    scale = (MLA_NOPE + MLA_ROPE) ** -0.5
    s = (jnp.einsum('bqhd,bkhd->bhqk', q_nope, k_nope, preferred_element_type=jnp.float32)
         + jnp.einsum('bqhr,bkr->bhqk', q_rope, k_rope, preferred_element_type=jnp.float32))
    p = jax.nn.softmax(s * scale, axis=-1)
    return jnp.einsum('bhqk,bkhd->bqhd', p.astype(v.dtype), v)


def mla_mixer(h_ctx, h_lat, rope, need_ctx, w_dq, g_q, w_uq, w_dkv, g_kv, w_ukv, w_o):
    B, C, _ = h_ctx.shape
    S = h_lat.shape[1]
    T = C + S
    h_all = jnp.concatenate([h_ctx, h_lat], axis=1)
    cq = rms_norm(h_all @ w_dq, g_q)
    q = (cq @ w_uq).reshape(B, T, MLA_HEADS, MLA_NOPE + MLA_ROPE)
    ckv = h_all @ w_dkv
    c_kv = rms_norm(ckv[..., :MLA_KV_RANK], g_kv)
    kv = (c_kv @ w_ukv).reshape(B, T, MLA_HEADS, MLA_NOPE + MLA_V)
    q_nope, q_rope = q[..., :MLA_NOPE], q[..., MLA_NOPE:]
    k_nope, v = kv[..., :MLA_NOPE], kv[..., MLA_NOPE:]
    k_rope = ckv[..., MLA_KV_RANK:]
    q_rope = jnp.concatenate([q_rope[:, :C], apply_axial_rope(q_rope[:, C:], rope)], axis=1)
    k_rope = jnp.concatenate([k_rope[:, :C], apply_axial_rope(k_rope[:, C:], rope)], axis=1)

    nb = S // Q_BLOCK
    def to_blocks(a):
        return jnp.moveaxis(a.reshape(B, nb, Q_BLOCK, *a.shape[2:]), 1, 0)
    qn_blk = to_blocks(q_nope[:, C:])
    qr_blk = to_blocks(q_rope[:, C:])
    o_lat = lax.map(lambda qb: mla_attend(qb[0], qb[1], k_nope, k_rope, v), (qn_blk, qr_blk))
    o_lat = jnp.moveaxis(o_lat, 0, 1).reshape(B, S, MLA_HEADS * MLA_V)
    y_lat = o_lat @ w_o

    y_ctx = None
    if need_ctx:
        o_ctx = mla_attend(q_nope[:, :C], q_rope[:, :C], k_nope[:, :C], k_rope[:, :C], v[:, :C])
        y_ctx = o_ctx.reshape(B, C, MLA_HEADS * MLA_V) @ w_o
    return y_ctx, y_lat


def gla_log_gate(h, w1, w2, b):
    return jax.nn.log_sigmoid(((h @ w1) @ w2 + b).astype(jnp.float32)) / GLA_TAU


def gla_chunk_scan(q, k, v, g, s0):
    B, T, H, _ = q.shape
    n = T // GLA_CHUNK
    def to_chunks(a):
        return a.reshape(B, n, GLA_CHUNK, H, a.shape[-1]).transpose(1, 0, 3, 2, 4).astype(jnp.float32)
    mask = jnp.tril(jnp.ones((GLA_CHUNK, GLA_CHUNK), dtype=bool))
    def step(s, inp):
        qc, kc, vc, gc = inp
        G = jnp.cumsum(gc, axis=2)
        G_last = G[:, :, -1:, :]
        q_t = qc * jnp.exp(G)
        k_t = kc * jnp.exp(-G)
        a = jnp.where(mask, jnp.einsum('bhid,bhjd->bhij', q_t, k_t), 0.0)
        o = jnp.einsum('bhij,bhjv->bhiv', a, vc) + jnp.einsum('bhid,bhdv->bhiv', q_t, s)
        k_end = kc * jnp.exp(G_last - G)
        s_new = jnp.exp(G_last[:, :, 0, :])[..., None] * s + jnp.einsum('bhjd,bhjv->bhdv', k_end, vc)
        return s_new, o
    s_fin, o = lax.scan(step, s0, (to_chunks(q), to_chunks(k), to_chunks(v), to_chunks(g)))
    o = o.transpose(1, 0, 3, 2, 4).reshape(B, T, H, v.shape[-1])
    return o, s_fin


def gla_mixer(h_ctx, h_lat, need_ctx, w_q, w_k, w_v, w_r, w_gate1, w_gate2, b_gate, g_o, w_o):
    B, C, _ = h_ctx.shape
    S = h_lat.shape[1]
    T = C + S
    h_all = jnp.concatenate([h_ctx, h_lat], axis=1)
    def heads(a, d):
        return a.reshape(B, T, GLA_HEADS, d)
    q = heads(h_all @ w_q, GLA_DK) * (GLA_DK ** -0.5)
    k = heads(h_all @ w_k, GLA_DK)
    v = heads(h_all @ w_v, GLA_DV)
    g_fwd = heads(gla_log_gate(h_all, w_gate1[0], w_gate2[0], b_gate[0]), GLA_DK)
    g_bwd = heads(gla_log_gate(h_all, w_gate1[1], w_gate2[1], b_gate[1]), GLA_DK)
    flip = lambda a: jnp.flip(a, axis=1)
    zeros = jnp.zeros((B, GLA_HEADS, GLA_DK, GLA_DV), jnp.float32)
    qc, kc, vc = q[:, :C], k[:, :C], v[:, :C]
    ql, kl, vl = q[:, C:], k[:, C:], v[:, C:]
    o_cf, s_cf = gla_chunk_scan(qc, kc, vc, g_fwd[:, :C], zeros)
    o_cb, s_cb = gla_chunk_scan(flip(qc), flip(kc), flip(vc), flip(g_bwd[:, :C]), zeros)
    o_lf, _ = gla_chunk_scan(ql, kl, vl, g_fwd[:, C:], s_cf)
    o_lb, _ = gla_chunk_scan(flip(ql), flip(kl), flip(vl), flip(g_bwd[:, C:]), s_cb)
    def out(o, h):
        y = rms_norm(o, g_o).astype(h.dtype).reshape(B, -1, GLA_HEADS * GLA_DV)
        return (y * jax.nn.silu(h @ w_r)) @ w_o
    y_lat = out(o_lf + flip(o_lb), h_lat)
    y_ctx = out(o_cf + flip(o_cb), h_ctx) if need_ctx else None
    return y_ctx, y_lat


def setup_inputs(seed: int = 0) -> dict:
    key = jax.random.key(seed)
    ks = jax.random.split(key, 32)
    D = D_MODEL
    n_a = (DEPTH + N_MIXERS - 1) // N_MIXERS
    n_b = DEPTH // N_MIXERS
    def nrm(k, shape, scale):
        return jax.random.normal(k, shape, jnp.float32) * scale
    return {
        'x': nrm(ks[0], (BATCH, SEQ, D), 1.0),
        'c': nrm(ks[1], (BATCH, D), 1.0),
        'ctx': nrm(ks[2], (BATCH, CTX_LEN, D), 1.0),
        'c_ctx': nrm(ks[3], (D,), 1.0),
        'ada_w': nrm(ks[4], (DEPTH, D, N_MOD * D), 0.5 * D ** -0.5),
        'ada_b': nrm(ks[5], (DEPTH, N_MOD * D), 0.02),
        'norm_g': 1.0 + nrm(ks[6], (DEPTH, 4, D), 0.02),
        'mlp_w1': nrm(ks[7], (DEPTH, D, D_FF), D ** -0.5),
        'mlp_w2': nrm(ks[8], (DEPTH, D_FF, D), D_FF ** -0.5),
        'mla_w_dq': nrm(ks[9], (n_a, D, MLA_Q_RANK), D ** -0.5),
        'mla_g_q': 1.0 + nrm(ks[10], (n_a, MLA_Q_RANK), 0.02),
        'mla_w_uq': nrm(ks[11], (n_a, MLA_Q_RANK, MLA_HEADS * (MLA_NOPE + MLA_ROPE)), MLA_Q_RANK ** -0.5),
        'mla_w_dkv': nrm(ks[12], (n_a, D, MLA_KV_RANK + MLA_ROPE), D ** -0.5),
        'mla_g_kv': 1.0 + nrm(ks[13], (n_a, MLA_KV_RANK), 0.02),
        'mla_w_ukv': nrm(ks[14], (n_a, MLA_KV_RANK, MLA_HEADS * (MLA_NOPE + MLA_V)), MLA_KV_RANK ** -0.5),
        'mla_w_o': nrm(ks[15], (n_a, MLA_HEADS * MLA_V, D), (MLA_HEADS * MLA_V) ** -0.5),
        'gla_w_q': nrm(ks[16], (n_b, D, GLA_HEADS * GLA_DK), D ** -0.5),
        'gla_w_k': nrm(ks[17], (n_b, D, GLA_HEADS * GLA_DK), D ** -0.5),
        'gla_w_v': nrm(ks[18], (n_b, D, GLA_HEADS * GLA_DV), D ** -0.5),
        'gla_w_r': nrm(ks[19], (n_b, D, GLA_HEADS * GLA_DV), D ** -0.5),
        'gla_w_gate1': nrm(ks[20], (n_b, 2, D, GLA_GATE_RANK), D ** -0.5),
        'gla_w_gate2': nrm(ks[21], (n_b, 2, GLA_GATE_RANK, GLA_HEADS * GLA_DK), GLA_GATE_RANK ** -0.5),
        'gla_b_gate': nrm(ks[22], (n_b, 2, GLA_HEADS * GLA_DK), 0.1),
        'gla_g_o': 1.0 + nrm(ks[23], (n_b, GLA_DV), 0.02),
        'gla_w_o': nrm(ks[24], (n_b, GLA_HEADS * GLA_DV, D), (GLA_HEADS * GLA_DV) ** -0.5),
    }


def reference(x, c, ctx, c_ctx, ada_w, ada_b, norm_g, mlp_w1, mlp_w2,
              mla_w_dq, mla_g_q, mla_w_uq, mla_w_dkv, mla_g_kv, mla_w_ukv, mla_w_o,
              gla_w_q, gla_w_k, gla_w_v, gla_w_r, gla_w_gate1, gla_w_gate2, gla_b_gate,
              gla_g_o, gla_w_o):
    B, S, D = x.shape
    rope = axial_rope_tables(S)
    silu_c = jax.nn.silu(c)
    silu_cc = jax.nn.silu(c_ctx)
    x_lat, x_ctx = x, ctx
    for i in range(DEPTH):
        need_ctx = i < DEPTH - 1
        j = i // N_MIXERS
        m_lat = (silu_c @ ada_w[i] + ada_b[i]).reshape(B, N_MOD, 1, D)
        m_ctx = (silu_cc @ ada_w[i] + ada_b[i]).reshape(N_MOD, D)
        h_lat = modulate(rms_norm(x_lat, norm_g[i, 0]), m_lat[:, 0], m_lat[:, 1])
        h_ctx = modulate(rms_norm(x_ctx, norm_g[i, 0]), m_ctx[0], m_ctx[1])
        if i % N_MIXERS == 0:
            y_ctx, y_lat = mla_mixer(h_ctx, h_lat, rope, need_ctx, mla_w_dq[j], mla_g_q[j], mla_w_uq[j],
                                     mla_w_dkv[j], mla_g_kv[j], mla_w_ukv[j], mla_w_o[j])
        else:
            y_ctx, y_lat = gla_mixer(h_ctx, h_lat, need_ctx, gla_w_q[j], gla_w_k[j], gla_w_v[j], gla_w_r[j],
                                     gla_w_gate1[j], gla_w_gate2[j], gla_b_gate[j], gla_g_o[j], gla_w_o[j])
        x_lat = x_lat + m_lat[:, 2] * rms_norm(y_lat, norm_g[i, 1])
        h = modulate(rms_norm(x_lat, norm_g[i, 2]), m_lat[:, 3], m_lat[:, 4])
        x_lat = x_lat + m_lat[:, 5] * rms_norm(sqrelu_mlp(h, mlp_w1[i], mlp_w2[i]), norm_g[i, 3])
        if need_ctx:
            x_ctx = x_ctx + m_ctx[2] * rms_norm(y_ctx, norm_g[i, 1])
            h = modulate(rms_norm(x_ctx, norm_g[i, 2]), m_ctx[3], m_ctx[4])
            x_ctx = x_ctx + m_ctx[5] * rms_norm(sqrelu_mlp(h, mlp_w1[i], mlp_w2[i]), norm_g[i, 3])
    return x_lat
```

```python
import functools
import math

import jax
import jax.numpy as jnp
from jax import lax
from jax.experimental import pallas as pl
from jax.experimental.pallas import tpu as pltpu

N_MOD = 6
NORM_EPS = 1e-6
GRID_W = 64
MLA_HEADS = 8
MLA_NOPE = 128
MLA_ROPE = 64
MLA_V = 128
MLA_Q_RANK = 384
MLA_KV_RANK = 256
ROPE_BASE = 10000.0
GLA_HEADS = 4
GLA_RANK = 16
GLA_TAU = 16.0
GLA_CHUNK = 64

LANES = 128
TILE = 256
MLA_QK = 2 * LANES
MIB = 1024 * 1024
LOG2E = 1.4426950408889634

F32 = jnp.float32
BF16 = jnp.bfloat16
NT = (((1,), (1,)), ((), ()))


def _dot(a, b):
    return jnp.dot(a, b, preferred_element_type=F32)


def _dot_nt(a, b):
    return lax.dot_general(a, b, NT, preferred_element_type=F32)


def _rms(x, g):
    ms = jnp.mean(x * x, axis=-1, keepdims=True)
    return x * lax.rsqrt(ms + NORM_EPS) * g


def _const_spec(shape):
    nd = len(shape)
    return pl.BlockSpec(shape, lambda *_: (0,) * nd, pipeline_mode=pl.Buffered(1))


def _params(sem, vmem_mib):
    return pltpu.CompilerParams(dimension_semantics=sem, vmem_limit_bytes=vmem_mib * MIB)


def _ada_kernel(c_ref, w_ref, b_ref, o_ref):
    c = c_ref[...]
    a = c / (1.0 + jnp.exp(-c))
    a_hi = a.astype(BF16)
    a_lo = (a - a_hi.astype(F32)).astype(BF16)
    w = w_ref[0]
    w_hi = w.astype(BF16)
    w_lo = (w - w_hi.astype(F32)).astype(BF16)
    o_ref[0] = _dot(a_hi, w_hi) + _dot(a_lo, w_hi) + _dot(a_hi, w_lo) + b_ref[0]


def _ada(cvec, ada_w, ada_b):
    depth, d, n = ada_w.shape
    rows = cvec.shape[0]
    tn = 1536
    return pl.pallas_call(
        _ada_kernel,
        out_shape=jax.ShapeDtypeStruct((depth, rows, n), F32),
        grid=(depth, n // tn),
        in_specs=[
            pl.BlockSpec((rows, d), lambda l, j: (0, 0)),
            pl.BlockSpec((1, d, tn), lambda l, j: (l, 0, j)),
            pl.BlockSpec((1, 1, tn), lambda l, j: (l, 0, j)),
        ],
        out_specs=pl.BlockSpec((1, rows, tn), lambda l, j: (l, 0, j)),
        compiler_params=_params(("parallel", "parallel"), 40),
        name="ada",
    )(cvec, ada_w, ada_b.reshape(depth, 1, n))


def _mla_proj_kernel(x_ref, mod_ref, ng_ref, wd_ref, gq_ref, gkv_ref, wqt_ref, wk_ref, wvt_ref,
                     rc_ref, rs_ref, rct_ref, rst_ref, qt_ref, k_ref, vt_ref, *, q_scale):
    x = x_ref[0]
    mod = mod_ref[0]
    h = (_rms(x, ng_ref[0:1]) * (1.0 + mod[1:2]) + mod[0:1]).astype(BF16)
    a = _dot(h, wd_ref[...])
    nq, nkv = MLA_Q_RANK, MLA_KV_RANK
    cq = _rms(a[:, :nq], gq_ref[...]).astype(BF16)
    ckv = _rms(a[:, nq:nq + nkv], gkv_ref[...]).astype(BF16)
    o = nq + nkv
    k_rope = a[:, o:o + LANES] * rc_ref[...] + a[:, o + LANES:o + 2 * LANES] * rs_ref[...]
    k_rope = k_rope.astype(BF16)

    hw = MLA_HEADS * LANES
    qt = _dot_nt(wqt_ref[...], cq) * q_scale
    k_nope = _dot(ckv, wk_ref[...]).astype(BF16)
    vt = _dot_nt(wvt_ref[...], ckv).astype(BF16)
    rct = rct_ref[...]
    rst = rst_ref[...]
    for hd in range(MLA_HEADS):
        s = hd * LANES
        qt_ref[0, hd, 0:LANES, :] = qt[s:s + LANES].astype(BF16)
        q_rope = qt[hw + s:hw + s + LANES] * rct + qt[2 * hw + s:2 * hw + s + LANES] * rst
        qt_ref[0, hd, LANES:2 * LANES, :] = q_rope.astype(BF16)
        k_ref[0, hd, :, 0:LANES] = k_nope[:, s:s + LANES]
        k_ref[0, hd, :, LANES:2 * LANES] = k_rope
        vt_ref[0, hd] = vt[s:s + LANES]


def _mla_proj(xa, mod, ng, wd, gq, gkv, wqt, wk, wvt, rc, rs, rct, rst):
    b, t, d = xa.shape
    nt = t // TILE
    nb = b
    hds = MLA_HEADS
    q_scale = (MLA_NOPE + MLA_ROPE) ** -0.5 * LOG2E
    mod_idx = lambda bi, i: (jnp.where(i == nt - 1, nb, bi), 0, 0)
    return pl.pallas_call(
        functools.partial(_mla_proj_kernel, q_scale=q_scale),
        out_shape=(
            jax.ShapeDtypeStruct((b, hds, MLA_QK, t), BF16),
            jax.ShapeDtypeStruct((b, hds, t, MLA_QK), BF16),
            jax.ShapeDtypeStruct((b, hds, MLA_V, t), BF16),
        ),
        grid=(b, nt),
        in_specs=[
            pl.BlockSpec((1, TILE, d), lambda bi, i: (bi, i, 0)),
            pl.BlockSpec((1, N_MOD, d), mod_idx),
            _const_spec(ng.shape),
            _const_spec(wd.shape),
            _const_spec(gq.shape),
            _const_spec(gkv.shape),
            _const_spec(wqt.shape),
            _const_spec(wk.shape),
            _const_spec(wvt.shape),
            pl.BlockSpec((TILE, LANES), lambda bi, i: (i, 0)),
            pl.BlockSpec((TILE, LANES), lambda bi, i: (i, 0)),
            pl.BlockSpec((LANES, TILE), lambda bi, i: (0, i)),
            pl.BlockSpec((LANES, TILE), lambda bi, i: (0, i)),
        ],
        out_specs=(
            pl.BlockSpec((1, hds, MLA_QK, TILE), lambda bi, i: (bi, 0, 0, i)),
            pl.BlockSpec((1, hds, TILE, MLA_QK), lambda bi, i: (bi, 0, i, 0)),
            pl.BlockSpec((1, hds, MLA_V, TILE), lambda bi, i: (bi, 0, 0, i)),
        ),
        compiler_params=_params(("parallel", "parallel"), 48),
        name="mla_proj",
    )(xa, mod, ng, wd, gq, gkv, wqt, wk, wvt, rc, rs, rct, rst)


def _attn_kernel(*refs, tk, n_chunks, aliased):
    if aliased:
        qt_ref, k_ref, vt_ref, _, o_ref = refs
    else:
        qt_ref, k_ref, vt_ref, o_ref = refs
    qt = qt_ref[0, 0]
    tq = qt.shape[1]

    def chunk(j, carry):
        m, l, acc = carry
        start = pl.multiple_of(j * tk, tk)
        k = k_ref[0, 0, pl.ds(start, tk), :]
        s = _dot(k, qt)
        m_new = jnp.maximum(m, jnp.max(s, axis=0, keepdims=True))
        alpha = jnp.exp2(m - m_new)
        p = jnp.exp2(s - m_new)
        l = alpha * l + jnp.sum(p, axis=0, keepdims=True)
        vt = vt_ref[0, 0, :, pl.ds(start, tk)]
        acc = alpha * acc + _dot(vt, p.astype(BF16))
        return m_new, l, acc

    init = (jnp.full((1, tq), -jnp.inf, F32), jnp.zeros((1, tq), F32), jnp.zeros((MLA_V, tq), F32))
    if n_chunks == 1:
        m, l, acc = chunk(0, init)
    else:
        m, l, acc = lax.fori_loop(0, n_chunks, chunk, init)
    o = acc * (1.0 / l)
    o_ref[0] = o.T.astype(o_ref.dtype)


def _attn_tk(n_keys):
    for tk in (768, 512, 256):
        if n_keys % tk == 0:
            return tk
    raise ValueError(f"unsupported key count {n_keys}")


def _mla_attention(qt, k, vt, n_lat, tq):
    b, hds, _, t = qt.shape
    out_shape = jax.ShapeDtypeStruct((b, t, hds * MLA_V), BF16)
    sem = ("parallel", "parallel", "arbitrary")
    tk = _attn_tk(t)
    o = pl.pallas_call(
        functools.partial(_attn_kernel, tk=tk, n_chunks=t // tk, aliased=False),
        out_shape=out_shape,
        grid=(b, hds, n_lat // tq),
        in_specs=[
            pl.BlockSpec((1, 1, MLA_QK, tq), lambda bi, h, i: (bi, h, 0, i)),
            pl.BlockSpec((1, 1, t, MLA_QK), lambda bi, h, i: (bi, h, 0, 0)),
            pl.BlockSpec((1, 1, MLA_V, t), lambda bi, h, i: (bi, h, 0, 0)),
        ],
        out_specs=pl.BlockSpec((1, tq, MLA_V), lambda bi, h, i: (bi, i, h)),
        compiler_params=_params(sem, 48),
        name="mla_attn_lat",
    )(qt, k, vt)
    ct = n_lat // TILE
    return pl.pallas_call(
        functools.partial(_attn_kernel, tk=TILE, n_chunks=1, aliased=True),
        out_shape=out_shape,
        grid=(b, hds, 1),
        in_specs=[
            pl.BlockSpec((1, 1, MLA_QK, TILE), lambda bi, h, i: (bi, h, 0, ct)),
            pl.BlockSpec((1, 1, TILE, MLA_QK), lambda bi, h, i: (bi, h, ct, 0)),
            pl.BlockSpec((1, 1, MLA_V, TILE), lambda bi, h, i: (bi, h, 0, ct)),
            pl.BlockSpec(memory_space=pl.ANY),
        ],
        out_specs=pl.BlockSpec((1, TILE, MLA_V), lambda bi, h, i: (bi, ct, h)),
        input_output_aliases={3: 0},
        compiler_params=_params(sem, 32),
        name="mla_attn_ctx",
    )(qt, k, vt, o)


def _gla_proj_kernel(x_ref, mod_ref, ng_ref, wg_ref, wvt_ref, w2_ref, bg_ref,
                     qf_ref, kf_ref, kfe_ref, qb_ref, kb_ref, kbe_ref, v_ref, vt_ref, r_ref, d_ref):
    x = x_ref[0]
    mod = mod_ref[0]
    h = (_rms(x, ng_ref[0:1]) * (1.0 + mod[1:2]) + mod[0:1]).astype(BF16)
    big = _dot(h, wg_ref[...])
    dk = GLA_HEADS * LANES
    dv = x.shape[1]
    q = big[:, :dk] * (LANES ** -0.5)
    k = big[:, dk:2 * dk]
    v_ref[0] = big[:, 2 * dk:2 * dk + dv].astype(BF16)
    rr = big[:, 2 * dk + dv:2 * dk + 2 * dv]
    r_ref[0] = (rr / (1.0 + jnp.exp(-rr))).astype(BF16)
    vt_ref[0] = _dot_nt(wvt_ref[...], h).astype(BF16)

    low = big[:, 2 * dk + 2 * dv:].astype(BF16)
    z = _dot(low, w2_ref[...]) + bg_ref[...]
    g = -(jnp.maximum(-z, 0.0) + jnp.log(1.0 + jnp.exp(-jnp.abs(z)))) * (1.0 / GLA_TAU)
    g_hi = g.astype(BF16)
    g_lo = (g - g_hi.astype(F32)).astype(BF16)

    row = lax.broadcasted_iota(jnp.int32, (TILE, TILE), 0)
    col = lax.broadcasted_iota(jnp.int32, (TILE, TILE), 1)
    shift = GLA_CHUNK.bit_length() - 1
    same = (row >> shift) == (col >> shift)
    prefix = jnp.where(same & (col <= row), 1.0, 0.0).astype(BF16)
    suffix = jnp.where(same & (col >= row), 1.0, 0.0).astype(BF16)
    total = jnp.where(same, 1.0, 0.0).astype(BF16)
    er = lax.broadcasted_iota(jnp.int32, (16, TILE), 0)
    ec = lax.broadcasted_iota(jnp.int32, (16, TILE), 1)
    pick = jnp.where(er == (ec >> shift), 1.0, 0.0).astype(BF16)

    def cums(mat, lo, hi):
        return _dot(mat, g_hi[:, lo:hi]) + _dot(mat, g_lo[:, lo:hi])

    gf = cums(prefix, 0, dk)
    tf = cums(total, 0, dk)
    gb = cums(suffix, dk, 2 * dk)
    tb = cums(total, dk, 2 * dk)
    d_ref[0, 0, :, 0:dk] = jnp.exp(cums(pick, 0, dk))
    d_ref[0, 0, :, dk:2 * dk] = jnp.exp(cums(pick, dk, 2 * dk))

    qf_ref[0] = (q * jnp.exp(gf)).astype(BF16)
    kf_ref[0] = (k * jnp.exp(-gf)).astype(BF16)
    kfe_ref[0] = (k * jnp.exp(tf - gf)).astype(BF16)
    qb_ref[0] = (q * jnp.exp(gb)).astype(BF16)
    kb_ref[0] = (k * jnp.exp(-gb)).astype(BF16)
    kbe_ref[0] = (k * jnp.exp(tb - gb)).astype(BF16)


def _gla_proj(xa, mod, ng, wg, wvt, w2, bg):
    b, t, d = xa.shape
    nt = t // TILE
    nb = b
    dk = GLA_HEADS * LANES
    mod_idx = lambda bi, i: (jnp.where(i == nt - 1, nb, bi), 0, 0)
    row_spec = lambda w: pl.BlockSpec((1, TILE, w), lambda bi, i: (bi, i, 0))
    qk = jax.ShapeDtypeStruct((b, t, dk), BF16)
    wide = jax.ShapeDtypeStruct((b, t, d), BF16)
    return pl.pallas_call(
        _gla_proj_kernel,
        out_shape=(qk, qk, qk, qk, qk, qk, wide,
                   jax.ShapeDtypeStruct((b, d, t), BF16), wide,
                   jax.ShapeDtypeStruct((b, nt, 16, 2 * dk), F32)),
        grid=(b, nt),
        in_specs=[
            row_spec(d),
            pl.BlockSpec((1, N_MOD, d), mod_idx),
            _const_spec(ng.shape),
            _const_spec(wg.shape),
            _const_spec(wvt.shape),
            _const_spec(w2.shape),
            _const_spec(bg.shape),
        ],
        out_specs=(row_spec(dk),) * 6 + (
            row_spec(d),
            pl.BlockSpec((1, d, TILE), lambda bi, i: (bi, 0, i)),
            row_spec(d),
            pl.BlockSpec((1, 1, 16, 2 * dk), lambda bi, i: (bi, i, 0, 0)),
        ),
        compiler_params=_params(("parallel", "parallel"), 48),
        name="gla_proj",
    )(xa, mod, ng, wg, wvt, w2, bg)


def _gla_scan_kernel(qf_ref, kf_ref, kfe_ref, vf_ref, vtf_ref, df_ref,
                     qb_ref, kb_ref, kbe_ref, vb_ref, vtb_ref, db_ref,
                     of_ref, ob_ref, s_ref):
    @pl.when(pl.program_id(1) == 0)
    def _():
        s_ref[...] = jnp.zeros(s_ref.shape, s_ref.dtype)

    L = GLA_CHUNK
    n_chunks = TILE // L
    dk = LANES
    dv = vf_ref.shape[2] // GLA_HEADS
    ri = lax.broadcasted_iota(jnp.int32, (L, L), 0)
    ci = lax.broadcasted_iota(jnp.int32, (L, L), 1)
    causal = ci <= ri
    anti = ci >= ri

    def run(direction, q_ref, k_ref, ke_ref, v_ref, vt_ref, d_ref, o_ref, mask, order):
        for c in order:
            rows = slice(c * L, (c + 1) * L)
            for hd in range(GLA_HEADS):
                kc = slice(hd * dk, (hd + 1) * dk)
                vc = slice(hd * dv, (hd + 1) * dv)
                q = q_ref[0, rows, kc]
                a = _dot_nt(q, k_ref[0, rows, kc])
                a = jnp.where(mask, a, 0.0).astype(BF16)
                idx = direction * GLA_HEADS + hd
                st = s_ref[idx]
                o = _dot(a, v_ref[0, rows, vc]) + _dot_nt(q, st.astype(BF16))
                o_ref[0, rows, vc] = o.astype(o_ref.dtype)
                decay = d_ref[0, 0, c:c + 1, direction * GLA_HEADS * dk + hd * dk:
                              direction * GLA_HEADS * dk + (hd + 1) * dk]
                s_ref[idx] = st * decay + _dot(vt_ref[0, vc, rows], ke_ref[0, rows, kc])

    run(0, qf_ref, kf_ref, kfe_ref, vf_ref, vtf_ref, df_ref, of_ref, causal, range(n_chunks))
    run(1, qb_ref, kb_ref, kbe_ref, vb_ref, vtb_ref, db_ref, ob_ref, anti, range(n_chunks - 1, -1, -1))


def _gla_scan(qf, kf, kfe, qb, kb, kbe, v, vt, dd):
    b, t, dkh = qf.shape
    d = v.shape[2]
    nt = t // TILE
    f_tile = lambda s: jnp.where(s == 0, nt - 1, s - 1)
    b_tile = lambda s: jnp.where(s == 0, nt - 1, nt - 1 - s)

    def specs(tile):
        row = lambda w: pl.BlockSpec((1, TILE, w), lambda bi, s: (bi, tile(s), 0))
        return [row(dkh), row(dkh), row(dkh), row(d),
                pl.BlockSpec((1, d, TILE), lambda bi, s: (bi, 0, tile(s))),
                pl.BlockSpec((1, 1, 16, 2 * dkh), lambda bi, s: (bi, tile(s), 0, 0))]

    out = jax.ShapeDtypeStruct((b, t, d), BF16)
    return pl.pallas_call(
        _gla_scan_kernel,
        out_shape=(out, out),
        grid=(b, nt),
        in_specs=specs(f_tile) + specs(b_tile),
        out_specs=(pl.BlockSpec((1, TILE, d), lambda bi, s: (bi, f_tile(s), 0)),
                   pl.BlockSpec((1, TILE, d), lambda bi, s: (bi, b_tile(s), 0))),
        scratch_shapes=[pltpu.VMEM((2 * GLA_HEADS, d // GLA_HEADS, LANES), F32)],
        compiler_params=_params(("parallel", "arbitrary"), 32),
        name="gla_scan",
    )(qf, kf, kfe, v, vt, dd, qb, kb, kbe, v, vt, dd)


def _post_kernel(*refs, gla):
    if gla:
        x_ref, mod_ref, ng_ref, of_ref, ob_ref, r_ref, go_ref, wo_ref, w1_ref, w2_ref, out_ref = refs
        o = of_ref[0].astype(F32) + ob_ref[0].astype(F32)
        dv = o.shape[1] // GLA_HEADS
        go = go_ref[...]
        o = jnp.concatenate([_rms(o[:, i * dv:(i + 1) * dv], go) for i in range(GLA_HEADS)], axis=1)
        y_in = (o * r_ref[0].astype(F32)).astype(BF16)
    else:
        x_ref, mod_ref, ng_ref, o_ref, wo_ref, w1_ref, w2_ref, out_ref = refs
        y_in = o_ref[0]
    x = x_ref[0]
    mod = mod_ref[0]
    ng = ng_ref[...]
    y = _dot(y_in, wo_ref[...])
    x1 = x + mod[2:3] * _rms(y, ng[1:2])
    h = (_rms(x1, ng[2:3]) * (1.0 + mod[4:5]) + mod[3:4]).astype(BF16)
    d = x.shape[1]
    z = jnp.zeros(x.shape, F32)
    for j in range(w1_ref.shape[1] // d):
        u = jnp.maximum(_dot(h, w1_ref[:, j * d:(j + 1) * d]), 0.0)
        z = z + _dot((u * u).astype(BF16), w2_ref[j * d:(j + 1) * d, :])
    out_ref[0] = x1 + mod[5:6] * _rms(z, ng[3:4])


def _post(xa, mod, ng, mixer_out, weights, gla):
    b, t, d = xa.shape
    nt = t // TILE
    nb = b
    mod_idx = lambda bi, i: (jnp.where(i == nt - 1, nb, bi), 0, 0)
    row = pl.BlockSpec((1, TILE, d), lambda bi, i: (bi, i, 0))
    return pl.pallas_call(
        functools.partial(_post_kernel, gla=gla),
        out_shape=jax.ShapeDtypeStruct(xa.shape, xa.dtype),
        grid=(b, nt),
        in_specs=[row, pl.BlockSpec((1, N_MOD, d), mod_idx), _const_spec(ng.shape)]
        + [row] * len(mixer_out) + [_const_spec(w.shape) for w in weights],
        out_specs=row,
        input_output_aliases={0: 0},
        compiler_params=_params(("parallel", "parallel"), 56),
        name="gla_post" if gla else "mla_post",
    )(xa, mod, ng, *mixer_out, *weights)


def _rope_tables(n_lat, n_ctx):
    t = jnp.arange(n_lat)
    row = (t // GRID_W).astype(F32)
    col = (t % GRID_W).astype(F32)
    half = MLA_ROPE // 2
    inv_freq = ROPE_BASE ** (-jnp.arange(0, half, 2, dtype=F32) / half)
    ang_r = row[:, None] * inv_freq
    ang_c = col[:, None] * inv_freq
    cr, sr, cc, sc = jnp.cos(ang_r), jnp.sin(ang_r), jnp.cos(ang_c), jnp.sin(ang_c)
    cos = jnp.concatenate([cr, cr, cc, cc], axis=1)
    sin = jnp.concatenate([-sr, sr, -sc, sc], axis=1)
    cos = jnp.concatenate([cos, jnp.ones((n_ctx, MLA_ROPE), F32)], axis=0)
    sin = jnp.concatenate([sin, jnp.zeros((n_ctx, MLA_ROPE), F32)], axis=0)
    pad = ((0, 0), (0, LANES - MLA_ROPE))
    return jnp.pad(cos, pad), jnp.pad(sin, pad)


def _swap_perm():
    q = MLA_ROPE // 4
    return jnp.array(list(range(q, 2 * q)) + list(range(0, q)) + list(range(3 * q, 4 * q)) + list(range(2 * q, 3 * q)))


def _mla_weights(w_dq, g_q, w_uq, w_dkv, g_kv, w_ukv, w_o):
    d = w_dq.shape[0]
    perm = _swap_perm()
    rope = w_dkv[:, MLA_KV_RANK:]
    zpad = jnp.zeros((d, LANES - MLA_ROPE), w_dkv.dtype)
    wd = jnp.concatenate([w_dq, w_dkv[:, :MLA_KV_RANK], rope, zpad, rope[:, perm], zpad], axis=1)
    uq = w_uq.reshape(MLA_Q_RANK, MLA_HEADS, MLA_NOPE + MLA_ROPE)
    q_nope = uq[:, :, :MLA_NOPE].reshape(MLA_Q_RANK, -1)
    q_rope = uq[:, :, MLA_NOPE:]
    hpad = ((0, 0), (0, 0), (0, LANES - MLA_ROPE))
    q_rope_p = jnp.pad(q_rope, hpad).reshape(MLA_Q_RANK, -1)
    q_swap_p = jnp.pad(q_rope[:, :, perm], hpad).reshape(MLA_Q_RANK, -1)
    wqt = jnp.concatenate([q_nope, q_rope_p, q_swap_p], axis=1).T
    ukv = w_ukv.reshape(MLA_KV_RANK, MLA_HEADS, MLA_NOPE + MLA_V)
    wk = ukv[:, :, :MLA_NOPE].reshape(MLA_KV_RANK, -1)
    wvt = ukv[:, :, MLA_NOPE:].reshape(MLA_KV_RANK, -1).T
    return (wd.astype(BF16), g_q.reshape(1, -1), g_kv.reshape(1, -1),
            wqt.astype(BF16), wk.astype(BF16), wvt.astype(BF16), w_o.astype(BF16))


def _gla_weights(w_q, w_k, w_v, w_r, w_gate1, w_gate2, b_gate, g_o, w_o):
    d = w_q.shape[0]
    dk = w_q.shape[1]
    low_pad = jnp.zeros((d, LANES - 2 * GLA_RANK), w_q.dtype)
    wg = jnp.concatenate([w_q, w_k, w_v, w_r, w_gate1[0], w_gate1[1], low_pad], axis=1)
    zero = jnp.zeros((GLA_RANK, dk), w_gate2.dtype)
    w2 = jnp.concatenate([
        jnp.concatenate([w_gate2[0], zero], axis=1),
        jnp.concatenate([zero, w_gate2[1]], axis=1),
        jnp.zeros((LANES - 2 * GLA_RANK, 2 * dk), w_gate2.dtype)], axis=0)
    bg = jnp.concatenate([b_gate[0], b_gate[1]]).reshape(1, -1)
    return (wg.astype(BF16), w_v.T.astype(BF16), w2.astype(BF16), bg,
            g_o.reshape(1, -1), w_o.astype(BF16))


def kernel(x, c, ctx, c_ctx, ada_w, ada_b, norm_g, mlp_w1, mlp_w2, mla_w_dq, mla_g_q, mla_w_uq, mla_w_dkv, mla_g_kv, mla_w_ukv, mla_w_o, gla_w_q, gla_w_k, gla_w_v, gla_w_r, gla_w_gate1, gla_w_gate2, gla_b_gate, gla_g_o, gla_w_o):
    b, s, d = x.shape
    n_ctx = ctx.shape[1]
    depth = ada_w.shape[0]
    assert n_ctx == TILE and s % TILE == 0 and d == GLA_HEADS * 2 * LANES

    xa = jnp.concatenate([x, ctx], axis=1)
    rows = -(-(b + 1) // 16) * 16
    cvec = jnp.concatenate([c, c_ctx[None], jnp.zeros((rows - b - 1, d), c.dtype)], axis=0)
    mod_all = _ada(cvec, ada_w, ada_b).reshape(depth, rows, N_MOD, d)

    rc, rs = _rope_tables(s, n_ctx)
    rct, rst = rc.T, rs.T
    tq = 512 if s % 512 == 0 else TILE

    for i in range(depth):
        j = i // 2
        mod = mod_all[i]
        ng = norm_g[i]
        w1 = mlp_w1[i].astype(BF16)
        w2 = mlp_w2[i].astype(BF16)
        if i % 2 == 0:
            wd, gq, gkv, wqt, wk, wvt, wo = _mla_weights(
                mla_w_dq[j], mla_g_q[j], mla_w_uq[j], mla_w_dkv[j], mla_g_kv[j], mla_w_ukv[j], mla_w_o[j])
            qt, k, vt = _mla_proj(xa, mod, ng, wd, gq, gkv, wqt, wk, wvt, rc, rs, rct, rst)
            o = _mla_attention(qt, k, vt, s, tq)
            xa = _post(xa, mod, ng, (o,), (wo, w1, w2), gla=False)
        else:
            wg, wvt, w2g, bg, go, wo = _gla_weights(
                gla_w_q[j], gla_w_k[j], gla_w_v[j], gla_w_r[j], gla_w_gate1[j], gla_w_gate2[j],
                gla_b_gate[j], gla_g_o[j], gla_w_o[j])
            qf, kf, kfe, qb, kb, kbe, v, vt, r, dd = _gla_proj(xa, mod, ng, wg, wvt, w2g, bg)
            of, ob = _gla_scan(qf, kf, kfe, qb, kb, kbe, v, vt, dd)
            xa = _post(xa, mod, ng, (of, ob, r), (go, wo, w1, w2), gla=True)
    return xa[:, :s]
```

```python
import functools
import math

import jax
import jax.numpy as jnp
from jax import lax
from jax.experimental import pallas as pl
from jax.experimental.pallas import tpu as pltpu

N_MOD = 6
NORM_EPS = 1e-6
GRID_W = 64
MLA_HEADS = 8
MLA_NOPE = 128
MLA_ROPE = 64
MLA_V = 128
MLA_Q_RANK = 384
MLA_KV_RANK = 256
ROPE_BASE = 10000.0
GLA_HEADS = 4
GLA_RANK = 16
GLA_TAU = 16.0
GLA_CHUNK = 64

LANES = 128
TILE = 256
MLA_QK = 2 * LANES
MIB = 1024 * 1024
LOG2E = 1.4426950408889634

F32 = jnp.float32
BF16 = jnp.bfloat16
NT = (((1,), (1,)), ((), ()))


def _dot(a, b):
    return jnp.dot(a, b, preferred_element_type=F32)


def _dot_nt(a, b):
    return lax.dot_general(a, b, NT, preferred_element_type=F32)


def _rms(x, g):
    ms = jnp.mean(x * x, axis=-1, keepdims=True)
    return x * lax.rsqrt(ms + NORM_EPS) * g


def _const_spec(shape):
    nd = len(shape)
    return pl.BlockSpec(shape, lambda *_: (0,) * nd, pipeline_mode=pl.Buffered(1))


def _params(sem, vmem_mib):
    return pltpu.CompilerParams(dimension_semantics=sem, vmem_limit_bytes=vmem_mib * MIB)


def _ada_kernel(c_ref, w_ref, b_ref, o_ref):
    c = c_ref[...]
    a = c / (1.0 + jnp.exp(-c))
    a_hi = a.astype(BF16)
    a_lo = (a - a_hi.astype(F32)).astype(BF16)
    w = w_ref[0]
    w_hi = w.astype(BF16)
    w_lo = (w - w_hi.astype(F32)).astype(BF16)
    o_ref[0] = _dot(a_hi, w_hi) + _dot(a_lo, w_hi) + _dot(a_hi, w_lo) + b_ref[0]


def _ada(cvec, ada_w, ada_b):
    depth, d, n = ada_w.shape
    rows = cvec.shape[0]
    tn = 1536
    return pl.pallas_call(
        _ada_kernel,
        out_shape=jax.ShapeDtypeStruct((depth, rows, n), F32),
        grid=(depth, n // tn),
        in_specs=[
            pl.BlockSpec((rows, d), lambda l, j: (0, 0)),
            pl.BlockSpec((1, d, tn), lambda l, j: (l, 0, j)),
            pl.BlockSpec((1, 1, tn), lambda l, j: (l, 0, j)),
        ],
        out_specs=pl.BlockSpec((1, rows, tn), lambda l, j: (l, 0, j)),
        compiler_params=_params(("parallel", "parallel"), 40),
        name="ada",
    )(cvec, ada_w, ada_b.reshape(depth, 1, n))


def _mla_proj_kernel(x_ref, mod_ref, ng_ref, wd_ref, gq_ref, gkv_ref, wqt_ref, wk_ref, wvt_ref,
                     rc_ref, rs_ref, rct_ref, rst_ref, qt_ref, k_ref, vt_ref, *, q_scale):
    x = x_ref[0]
    mod = mod_ref[0]
    h = (_rms(x, ng_ref[0:1]) * (1.0 + mod[1:2]) + mod[0:1]).astype(BF16)
    a = _dot(h, wd_ref[...])
    nq, nkv = MLA_Q_RANK, MLA_KV_RANK
    cq = _rms(a[:, :nq], gq_ref[...]).astype(BF16)
    ckv = _rms(a[:, nq:nq + nkv], gkv_ref[...]).astype(BF16)
    o = nq + nkv
    k_rope = a[:, o:o + LANES] * rc_ref[...] + a[:, o + LANES:o + 2 * LANES] * rs_ref[...]
    k_rope = k_rope.astype(BF16)

    hw = MLA_HEADS * LANES
    qt = _dot_nt(wqt_ref[...], cq) * q_scale
    k_nope = _dot(ckv, wk_ref[...]).astype(BF16)
    vt = _dot_nt(wvt_ref[...], ckv).astype(BF16)
    rct = rct_ref[...]
    rst = rst_ref[...]
    for hd in range(MLA_HEADS):
        s = hd * LANES
        qt_ref[0, hd, 0:LANES, :] = qt[s:s + LANES].astype(BF16)
        q_rope = qt[hw + s:hw + s + LANES] * rct + qt[2 * hw + s:2 * hw + s + LANES] * rst
        qt_ref[0, hd, LANES:2 * LANES, :] = q_rope.astype(BF16)
        k_ref[0, hd, :, 0:LANES] = k_nope[:, s:s + LANES]
        k_ref[0, hd, :, LANES:2 * LANES] = k_rope
        vt_ref[0, hd] = vt[s:s + LANES]


def _mla_proj(xa, mod, ng, wd, gq, gkv, wqt, wk, wvt, rc, rs, rct, rst):
    b, t, d = xa.shape
    nt = t // TILE
    nb = b
    hds = MLA_HEADS
    q_scale = (MLA_NOPE + MLA_ROPE) ** -0.5 * LOG2E
    mod_idx = lambda bi, i: (jnp.where(i == nt - 1, nb, bi), 0, 0)
    return pl.pallas_call(
        functools.partial(_mla_proj_kernel, q_scale=q_scale),
        out_shape=(
            jax.ShapeDtypeStruct((b, hds, MLA_QK, t), BF16),
            jax.ShapeDtypeStruct((b, hds, t, MLA_QK), BF16),
            jax.ShapeDtypeStruct((b, hds, MLA_V, t), BF16),
        ),
        grid=(b, nt),
        in_specs=[
            pl.BlockSpec((1, TILE, d), lambda bi, i: (bi, i, 0)),
            pl.BlockSpec((1, N_MOD, d), mod_idx),
            _const_spec(ng.shape),
            _const_spec(wd.shape),
            _const_spec(gq.shape),
            _const_spec(gkv.shape),
            _const_spec(wqt.shape),
            _const_spec(wk.shape),
            _const_spec(wvt.shape),
            pl.BlockSpec((TILE, LANES), lambda bi, i: (i, 0)),
            pl.BlockSpec((TILE, LANES), lambda bi, i: (i, 0)),
            pl.BlockSpec((LANES, TILE), lambda bi, i: (0, i)),
            pl.BlockSpec((LANES, TILE), lambda bi, i: (0, i)),
        ],
        out_specs=(
            pl.BlockSpec((1, hds, MLA_QK, TILE), lambda bi, i: (bi, 0, 0, i)),
            pl.BlockSpec((1, hds, TILE, MLA_QK), lambda bi, i: (bi, 0, i, 0)),
            pl.BlockSpec((1, hds, MLA_V, TILE), lambda bi, i: (bi, 0, 0, i)),
        ),
        compiler_params=_params(("parallel", "parallel"), 48),
        name="mla_proj",
    )(xa, mod, ng, wd, gq, gkv, wqt, wk, wvt, rc, rs, rct, rst)


def _attn_kernel(*refs, tk, n_chunks, aliased):
    if aliased:
        qt_ref, k_ref, vt_ref, _, o_ref, s_ref = refs
    else:
        qt_ref, k_ref, vt_ref, o_ref, s_ref = refs
    tq = qt_ref.shape[3]

    def scores(j, slot):
        start = pl.multiple_of(j * tk, tk)
        k = k_ref[0, 0, pl.ds(start, tk), :]
        s_ref[slot] = _dot(k, qt_ref[0, 0])

    def update(j, slot, carry):
        m, l, acc = carry
        s = s_ref[slot]
        m_new = jnp.maximum(m, jnp.max(s, axis=0, keepdims=True))
        alpha = jnp.exp2(m - m_new)
        p = jnp.exp2(s - m_new)
        l = alpha * l + jnp.sum(p, axis=0, keepdims=True)
        start = pl.multiple_of(j * tk, tk)
        vt = vt_ref[0, 0, :, pl.ds(start, tk)]
        acc = alpha * acc + _dot(vt, p.astype(BF16))
        return m_new, l, acc

    def pair(jj, carry):
        j = 2 * jj
        scores(j + 1, 1)
        carry = update(j, 0, carry)
        scores(j + 2, 0)
        return update(j + 1, 1, carry)

    carry = (jnp.full((1, tq), -jnp.inf, F32), jnp.zeros((1, tq), F32), jnp.zeros((MLA_V, tq), F32))
    scores(0, 0)
    n_pairs = (n_chunks - 1) // 2
    if n_pairs > 0:
        carry = lax.fori_loop(0, n_pairs, pair, carry)
    if n_chunks % 2 == 0:
        scores(n_chunks - 1, 1)
        carry = update(n_chunks - 2, 0, carry)
        carry = update(n_chunks - 1, 1, carry)
    else:
        carry = update(n_chunks - 1, 0, carry)
    m, l, acc = carry
    o = acc * (1.0 / l)
    o_ref[0] = o.T.astype(o_ref.dtype)


def _attn_tk(n_keys):
    for tk in (768, 512, 256):
        if n_keys % tk == 0:
            return tk
    raise ValueError(f"unsupported key count {n_keys}")


def _mla_attention(qt, k, vt, n_lat, tq):
    b, hds, _, t = qt.shape
    out_shape = jax.ShapeDtypeStruct((b, t, hds * MLA_V), BF16)
    sem = ("parallel", "parallel", "arbitrary")
    tk = _attn_tk(t)
    o = pl.pallas_call(
        functools.partial(_attn_kernel, tk=tk, n_chunks=t // tk, aliased=False),
        out_shape=out_shape,
        grid=(b, hds, n_lat // tq),
        in_specs=[
            pl.BlockSpec((1, 1, MLA_QK, tq), lambda bi, h, i: (bi, h, 0, i)),
            pl.BlockSpec((1, 1, t, MLA_QK), lambda bi, h, i: (bi, h, 0, 0)),
            pl.BlockSpec((1, 1, MLA_V, t), lambda bi, h, i: (bi, h, 0, 0)),
        ],
        out_specs=pl.BlockSpec((1, tq, MLA_V), lambda bi, h, i: (bi, i, h)),
        scratch_shapes=[pltpu.VMEM((2, tk, tq), F32)],
        compiler_params=_params(sem, 48),
        name="mla_attn_lat",
    )(qt, k, vt)
    ct = n_lat // TILE
    return pl.pallas_call(
        functools.partial(_attn_kernel, tk=TILE, n_chunks=1, aliased=True),
        out_shape=out_shape,
        grid=(b, hds, 1),
        in_specs=[
            pl.BlockSpec((1, 1, MLA_QK, TILE), lambda bi, h, i: (bi, h, 0, ct)),
            pl.BlockSpec((1, 1, TILE, MLA_QK), lambda bi, h, i: (bi, h, ct, 0)),
            pl.BlockSpec((1, 1, MLA_V, TILE), lambda bi, h, i: (bi, h, 0, ct)),
            pl.BlockSpec(memory_space=pl.ANY),
        ],
        out_specs=pl.BlockSpec((1, TILE, MLA_V), lambda bi, h, i: (bi, ct, h)),
        input_output_aliases={3: 0},
        scratch_shapes=[pltpu.VMEM((2, TILE, TILE), F32)],
        compiler_params=_params(sem, 32),
        name="mla_attn_ctx",
    )(qt, k, vt, o)


def _gla_proj_kernel(x_ref, mod_ref, ng_ref, wg_ref, wvt_ref, w2_ref, bg_ref,
                     qf_ref, kf_ref, kfe_ref, qb_ref, kb_ref, kbe_ref, v_ref, vt_ref, r_ref, d_ref):
    x = x_ref[0]
    mod = mod_ref[0]
    h = (_rms(x, ng_ref[0:1]) * (1.0 + mod[1:2]) + mod[0:1]).astype(BF16)
    big = _dot(h, wg_ref[...])
    dk = GLA_HEADS * LANES
    dv = x.shape[1]
    q = big[:, :dk] * (LANES ** -0.5)
    k = big[:, dk:2 * dk]
    v_ref[0] = big[:, 2 * dk:2 * dk + dv].astype(BF16)
    rr = big[:, 2 * dk + dv:2 * dk + 2 * dv]
    r_ref[0] = (rr / (1.0 + jnp.exp(-rr))).astype(BF16)
    vt_ref[0] = _dot_nt(wvt_ref[...], h).astype(BF16)

    low = big[:, 2 * dk + 2 * dv:].astype(BF16)
    z = _dot(low, w2_ref[...]) + bg_ref[...]
    g = -(jnp.maximum(-z, 0.0) + jnp.log(1.0 + jnp.exp(-jnp.abs(z)))) * (1.0 / GLA_TAU)
    g_hi = g.astype(BF16)
    g_lo = (g - g_hi.astype(F32)).astype(BF16)

    row = lax.broadcasted_iota(jnp.int32, (TILE, TILE), 0)
    col = lax.broadcasted_iota(jnp.int32, (TILE, TILE), 1)
    shift = GLA_CHUNK.bit_length() - 1
    same = (row >> shift) == (col >> shift)
    prefix = jnp.where(same & (col <= row), 1.0, 0.0).astype(BF16)
    suffix = jnp.where(same & (col >= row), 1.0, 0.0).astype(BF16)
    total = jnp.where(same, 1.0, 0.0).astype(BF16)
    er = lax.broadcasted_iota(jnp.int32, (16, TILE), 0)
    ec = lax.broadcasted_iota(jnp.int32, (16, TILE), 1)
    pick = jnp.where(er == (ec >> shift), 1.0, 0.0).astype(BF16)

    def cums(mat, lo, hi):
        return _dot(mat, g_hi[:, lo:hi]) + _dot(mat, g_lo[:, lo:hi])

    gf = cums(prefix, 0, dk)
    tf = cums(total, 0, dk)
    gb = cums(suffix, dk, 2 * dk)
    tb = cums(total, dk, 2 * dk)
    d_ref[0, 0, :, 0:dk] = jnp.exp(cums(pick, 0, dk))
    d_ref[0, 0, :, dk:2 * dk] = jnp.exp(cums(pick, dk, 2 * dk))

    qf_ref[0] = (q * jnp.exp(gf)).astype(BF16)
    kf_ref[0] = (k * jnp.exp(-gf)).astype(BF16)
    kfe_ref[0] = (k * jnp.exp(tf - gf)).astype(BF16)
    qb_ref[0] = (q * jnp.exp(gb)).astype(BF16)
    kb_ref[0] = (k * jnp.exp(-gb)).astype(BF16)
    kbe_ref[0] = (k * jnp.exp(tb - gb)).astype(BF16)


def _gla_proj(xa, mod, ng, wg, wvt, w2, bg):
    b, t, d = xa.shape
    nt = t // TILE
    nb = b
    dk = GLA_HEADS * LANES
    mod_idx = lambda bi, i: (jnp.where(i == nt - 1, nb, bi), 0, 0)
    row_spec = lambda w: pl.BlockSpec((1, TILE, w), lambda bi, i: (bi, i, 0))
    qk = jax.ShapeDtypeStruct((b, t, dk), BF16)
    wide = jax.ShapeDtypeStruct((b, t, d), BF16)
    return pl.pallas_call(
        _gla_proj_kernel,
        out_shape=(qk, qk, qk, qk, qk, qk, wide,
                   jax.ShapeDtypeStruct((b, d, t), BF16), wide,
                   jax.ShapeDtypeStruct((b, nt, 16, 2 * dk), F32)),
        grid=(b, nt),
        in_specs=[
            row_spec(d),
            pl.BlockSpec((1, N_MOD, d), mod_idx),
            _const_spec(ng.shape),
            _const_spec(wg.shape),
            _const_spec(wvt.shape),
            _const_spec(w2.shape),
            _const_spec(bg.shape),
        ],
        out_specs=(row_spec(dk),) * 6 + (
            row_spec(d),
            pl.BlockSpec((1, d, TILE), lambda bi, i: (bi, 0, i)),
            row_spec(d),
            pl.BlockSpec((1, 1, 16, 2 * dk), lambda bi, i: (bi, i, 0, 0)),
        ),
        compiler_params=_params(("parallel", "parallel"), 48),
        name="gla_proj",
    )(xa, mod, ng, wg, wvt, w2, bg)


def _gla_scan_kernel(qf_ref, kf_ref, kfe_ref, vf_ref, vtf_ref, df_ref,
                     qb_ref, kb_ref, kbe_ref, vb_ref, vtb_ref, db_ref,
                     of_ref, ob_ref, s_ref):
    @pl.when(pl.program_id(1) == 0)
    def _():
        s_ref[...] = jnp.zeros(s_ref.shape, s_ref.dtype)

    L = GLA_CHUNK
    n_chunks = TILE // L
    dk = LANES
    dv = vf_ref.shape[2] // GLA_HEADS
    ri = lax.broadcasted_iota(jnp.int32, (L, L), 0)
    ci = lax.broadcasted_iota(jnp.int32, (L, L), 1)
    causal = ci <= ri
    anti = ci >= ri

    def run(direction, q_ref, k_ref, ke_ref, v_ref, vt_ref, d_ref, o_ref, mask, order):
        for c in order:
            rows = slice(c * L, (c + 1) * L)
            for hd in range(GLA_HEADS):
                kc = slice(hd * dk, (hd + 1) * dk)
                vc = slice(hd * dv, (hd + 1) * dv)
                q = q_ref[0, rows, kc]
                a = _dot_nt(q, k_ref[0, rows, kc])
                a = jnp.where(mask, a, 0.0).astype(BF16)
                idx = direction * GLA_HEADS + hd
                st = s_ref[idx]
                o = _dot(a, v_ref[0, rows, vc]) + _dot_nt(q, st.astype(BF16))
                o_ref[0, rows, vc] = o.astype(o_ref.dtype)
                decay = d_ref[0, 0, c:c + 1, direction * GLA_HEADS * dk + hd * dk:
                              direction * GLA_HEADS * dk + (hd + 1) * dk]
                s_ref[idx] = st * decay + _dot(vt_ref[0, vc, rows], ke_ref[0, rows, kc])

    run(0, qf_ref, kf_ref, kfe_ref, vf_ref, vtf_ref, df_ref, of_ref, causal, range(n_chunks))
    run(1, qb_ref, kb_ref, kbe_ref, vb_ref, vtb_ref, db_ref, ob_ref, anti, range(n_chunks - 1, -1, -1))


def _gla_scan(qf, kf, kfe, qb, kb, kbe, v, vt, dd):
    b, t, dkh = qf.shape
    d = v.shape[2]
    nt = t // TILE
    f_tile = lambda s: jnp.where(s == 0, nt - 1, s - 1)
    b_tile = lambda s: jnp.where(s == 0, nt - 1, nt - 1 - s)

    def specs(tile):
        row = lambda w: pl.BlockSpec((1, TILE, w), lambda bi, s: (bi, tile(s), 0))
        return [row(dkh), row(dkh), row(dkh), row(d),
                pl.BlockSpec((1, d, TILE), lambda bi, s: (bi, 0, tile(s))),
                pl.BlockSpec((1, 1, 16, 2 * dkh), lambda bi, s: (bi, tile(s), 0, 0))]

    out = jax.ShapeDtypeStruct((b, t, d), BF16)
    return pl.pallas_call(
        _gla_scan_kernel,
        out_shape=(out, out),
        grid=(b, nt),
        in_specs=specs(f_tile) + specs(b_tile),
        out_specs=(pl.BlockSpec((1, TILE, d), lambda bi, s: (bi, f_tile(s), 0)),
                   pl.BlockSpec((1, TILE, d), lambda bi, s: (bi, b_tile(s), 0))),
        scratch_shapes=[pltpu.VMEM((2 * GLA_HEADS, d // GLA_HEADS, LANES), F32)],
        compiler_params=_params(("parallel", "arbitrary"), 32),
        name="gla_scan",
    )(qf, kf, kfe, v, vt, dd, qb, kb, kbe, v, vt, dd)


def _post_kernel(*refs, gla):
    if gla:
        x_ref, mod_ref, ng_ref, of_ref, ob_ref, r_ref, go_ref, wo_ref, w1_ref, w2_ref, out_ref = refs
        o = of_ref[0].astype(F32) + ob_ref[0].astype(F32)
        dv = o.shape[1] // GLA_HEADS
        go = go_ref[...]
        o = jnp.concatenate([_rms(o[:, i * dv:(i + 1) * dv], go) for i in range(GLA_HEADS)], axis=1)
        y_in = (o * r_ref[0].astype(F32)).astype(BF16)
    else:
        x_ref, mod_ref, ng_ref, o_ref, wo_ref, w1_ref, w2_ref, out_ref = refs
        y_in = o_ref[0]
    x = x_ref[0]
    mod = mod_ref[0]
    ng = ng_ref[...]
    y = _dot(y_in, wo_ref[...])
    x1 = x + mod[2:3] * _rms(y, ng[1:2])
    h = (_rms(x1, ng[2:3]) * (1.0 + mod[4:5]) + mod[3:4]).astype(BF16)
    d = x.shape[1]
    z = jnp.zeros(x.shape, F32)
    for j in range(w1_ref.shape[1] // d):
        u = jnp.maximum(_dot(h, w1_ref[:, j * d:(j + 1) * d]), 0.0)
        z = z + _dot((u * u).astype(BF16), w2_ref[j * d:(j + 1) * d, :])
    out_ref[0] = x1 + mod[5:6] * _rms(z, ng[3:4])


def _post(xa, mod, ng, mixer_out, weights, gla):
    b, t, d = xa.shape
    nt = t // TILE
    nb = b
    mod_idx = lambda bi, i: (jnp.where(i == nt - 1, nb, bi), 0, 0)
    row = pl.BlockSpec((1, TILE, d), lambda bi, i: (bi, i, 0))
    return pl.pallas_call(
        functools.partial(_post_kernel, gla=gla),
        out_shape=jax.ShapeDtypeStruct(xa.shape, xa.dtype),
        grid=(b, nt),
        in_specs=[row, pl.BlockSpec((1, N_MOD, d), mod_idx), _const_spec(ng.shape)]
        + [row] * len(mixer_out) + [_const_spec(w.shape) for w in weights],
        out_specs=row,
        input_output_aliases={0: 0},
        compiler_params=_params(("parallel", "parallel"), 56),
        name="gla_post" if gla else "mla_post",
    )(xa, mod, ng, *mixer_out, *weights)


def _rope_tables(n_lat, n_ctx):
    t = jnp.arange(n_lat)
    row = (t // GRID_W).astype(F32)
    col = (t % GRID_W).astype(F32)
    half = MLA_ROPE // 2
    inv_freq = ROPE_BASE ** (-jnp.arange(0, half, 2, dtype=F32) / half)
    ang_r = row[:, None] * inv_freq
    ang_c = col[:, None] * inv_freq
    cr, sr, cc, sc = jnp.cos(ang_r), jnp.sin(ang_r), jnp.cos(ang_c), jnp.sin(ang_c)
    cos = jnp.concatenate([cr, cr, cc, cc], axis=1)
    sin = jnp.concatenate([-sr, sr, -sc, sc], axis=1)
    cos = jnp.concatenate([cos, jnp.ones((n_ctx, MLA_ROPE), F32)], axis=0)
    sin = jnp.concatenate([sin, jnp.zeros((n_ctx, MLA_ROPE), F32)], axis=0)
    pad = ((0, 0), (0, LANES - MLA_ROPE))
    return jnp.pad(cos, pad), jnp.pad(sin, pad)


def _swap_perm():
    q = MLA_ROPE // 4
    return jnp.array(list(range(q, 2 * q)) + list(range(0, q)) + list(range(3 * q, 4 * q)) + list(range(2 * q, 3 * q)))


def _mla_weights(w_dq, g_q, w_uq, w_dkv, g_kv, w_ukv, w_o):
    d = w_dq.shape[0]
    perm = _swap_perm()
    rope = w_dkv[:, MLA_KV_RANK:]
    zpad = jnp.zeros((d, LANES - MLA_ROPE), w_dkv.dtype)
    wd = jnp.concatenate([w_dq, w_dkv[:, :MLA_KV_RANK], rope, zpad, rope[:, perm], zpad], axis=1)
    uq = w_uq.reshape(MLA_Q_RANK, MLA_HEADS, MLA_NOPE + MLA_ROPE)
    q_nope = uq[:, :, :MLA_NOPE].reshape(MLA_Q_RANK, -1)
    q_rope = uq[:, :, MLA_NOPE:]
    hpad = ((0, 0), (0, 0), (0, LANES - MLA_ROPE))
    q_rope_p = jnp.pad(q_rope, hpad).reshape(MLA_Q_RANK, -1)
    q_swap_p = jnp.pad(q_rope[:, :, perm], hpad).reshape(MLA_Q_RANK, -1)
    wqt = jnp.concatenate([q_nope, q_rope_p, q_swap_p], axis=1).T
    ukv = w_ukv.reshape(MLA_KV_RANK, MLA_HEADS, MLA_NOPE + MLA_V)
    wk = ukv[:, :, :MLA_NOPE].reshape(MLA_KV_RANK, -1)
    wvt = ukv[:, :, MLA_NOPE:].reshape(MLA_KV_RANK, -1).T
    return (wd.astype(BF16), g_q.reshape(1, -1), g_kv.reshape(1, -1),
            wqt.astype(BF16), wk.astype(BF16), wvt.astype(BF16), w_o.astype(BF16))


def _gla_weights(w_q, w_k, w_v, w_r, w_gate1, w_gate2, b_gate, g_o, w_o):
    d = w_q.shape[0]
    dk = w_q.shape[1]
    low_pad = jnp.zeros((d, LANES - 2 * GLA_RANK), w_q.dtype)
    wg = jnp.concatenate([w_q, w_k, w_v, w_r, w_gate1[0], w_gate1[1], low_pad], axis=1)
    zero = jnp.zeros((GLA_RANK, dk), w_gate2.dtype)
    w2 = jnp.concatenate([
        jnp.concatenate([w_gate2[0], zero], axis=1),
        jnp.concatenate([zero, w_gate2[1]], axis=1),
        jnp.zeros((LANES - 2 * GLA_RANK, 2 * dk), w_gate2.dtype)], axis=0)
    bg = jnp.concatenate([b_gate[0], b_gate[1]]).reshape(1, -1)
    return (wg.astype(BF16), w_v.T.astype(BF16), w2.astype(BF16), bg,
            g_o.reshape(1, -1), w_o.astype(BF16))


def kernel(x, c, ctx, c_ctx, ada_w, ada_b, norm_g, mlp_w1, mlp_w2, mla_w_dq, mla_g_q, mla_w_uq, mla_w_dkv, mla_g_kv, mla_w_ukv, mla_w_o, gla_w_q, gla_w_k, gla_w_v, gla_w_r, gla_w_gate1, gla_w_gate2, gla_b_gate, gla_g_o, gla_w_o):
    b, s, d = x.shape
    n_ctx = ctx.shape[1]
    depth = ada_w.shape[0]
    assert n_ctx == TILE and s % TILE == 0 and d == GLA_HEADS * 2 * LANES

    xa = jnp.concatenate([x, ctx], axis=1)
    rows = -(-(b + 1) // 16) * 16
    cvec = jnp.concatenate([c, c_ctx[None], jnp.zeros((rows - b - 1, d), c.dtype)], axis=0)
    mod_all = _ada(cvec, ada_w, ada_b).reshape(depth, rows, N_MOD, d)

    rc, rs = _rope_tables(s, n_ctx)
    rct, rst = rc.T, rs.T
    tq = 2048 if s % 2048 == 0 else TILE

    for i in range(depth):
        j = i // 2
        mod = mod_all[i]
        ng = norm_g[i]
        w1 = mlp_w1[i].astype(BF16)
        w2 = mlp_w2[i].astype(BF16)
        if i % 2 == 0:
            wd, gq, gkv, wqt, wk, wvt, wo = _mla_weights(
                mla_w_dq[j], mla_g_q[j], mla_w_uq[j], mla_w_dkv[j], mla_g_kv[j], mla_w_ukv[j], mla_w_o[j])
            qt, k, vt = _mla_proj(xa, mod, ng, wd, gq, gkv, wqt, wk, wvt, rc, rs, rct, rst)
            o = _mla_attention(qt, k, vt, s, tq)
            xa = _post(xa, mod, ng, (o,), (wo, w1, w2), gla=False)
        else:
            wg, wvt, w2g, bg, go, wo = _gla_weights(
                gla_w_q[j], gla_w_k[j], gla_w_v[j], gla_w_r[j], gla_w_gate1[j], gla_w_gate2[j],
                gla_b_gate[j], gla_g_o[j], gla_w_o[j])
            qf, kf, kfe, qb, kb, kbe, v, vt, r, dd = _gla_proj(xa, mod, ng, wg, wvt, w2g, bg)
            of, ob = _gla_scan(qf, kf, kfe, qb, kb, kbe, v, vt, dd)
            xa = _post(xa, mod, ng, (of, ob, r), (go, wo, w1, w2), gla=True)
    return xa[:, :s]
```

```python
import functools
import math

import jax
import jax.numpy as jnp
from jax import lax
from jax.experimental import pallas as pl
from jax.experimental.pallas import tpu as pltpu

N_MOD = 6
NORM_EPS = 1e-6
GRID_W = 64
MLA_HEADS = 8
MLA_NOPE = 128
MLA_ROPE = 64
MLA_V = 128
MLA_VX = MLA_V + 16
MLA_Q_RANK = 384
MLA_KV_RANK = 256
ROPE_BASE = 10000.0
GLA_HEADS = 4
GLA_RANK = 16
GLA_TAU = 16.0
GLA_CHUNK = 64

LANES = 128
TILE = 256
MLA_QK = 2 * LANES
GLA_DECAY_ROWS = 8
MIB = 1024 * 1024
LOG2E = 1.4426950408889634

F32 = jnp.float32
BF16 = jnp.bfloat16
NT = (((1,), (1,)), ((), ()))


def _dot(a, b):
    return jnp.dot(a, b, preferred_element_type=F32)


def _dot_nt(a, b):
    return lax.dot_general(a, b, NT, preferred_element_type=F32)


def _rms(x, g):
    ms = jnp.mean(x * x, axis=-1, keepdims=True)
    return x * lax.rsqrt(ms + NORM_EPS) * g


def _const_spec(shape):
    nd = len(shape)
    return pl.BlockSpec(shape, lambda *_: (0,) * nd, pipeline_mode=pl.Buffered(1))


def _params(sem, vmem_mib):
    return pltpu.CompilerParams(dimension_semantics=sem, vmem_limit_bytes=vmem_mib * MIB)


def _stream_specs(stream, n_tiles, d):
    if len(stream) == 1:
        return [pl.BlockSpec((1, TILE, d), lambda bi, i: (bi, i, 0))]
    last_lat = n_tiles - 2
    return [pl.BlockSpec((1, TILE, d), lambda bi, i: (bi, jnp.minimum(i, last_lat), 0)),
            pl.BlockSpec((1, TILE, d), lambda bi, i: (bi, 0, 0))]


def _stream_tile(stream_refs):
    if len(stream_refs) == 1:
        return stream_refs[0][0]
    is_ctx = pl.program_id(1) == pl.num_programs(1) - 1
    return jnp.where(is_ctx, stream_refs[1][0], stream_refs[0][0])


def _ada_kernel(c_ref, w_ref, b_ref, o_ref):
    c = c_ref[...]
    a = c / (1.0 + jnp.exp(-c))
    a_hi = a.astype(BF16)
    a_lo = (a - a_hi.astype(F32)).astype(BF16)
    w = w_ref[0]
    w_hi = w.astype(BF16)
    w_lo = (w - w_hi.astype(F32)).astype(BF16)
    o_ref[0] = _dot(a_hi, w_hi) + _dot(a_lo, w_hi) + _dot(a_hi, w_lo) + b_ref[0]


def _ada(cvec, ada_w, ada_b):
    depth, d, n = ada_w.shape
    rows = cvec.shape[0]
    tn = 1536
    return pl.pallas_call(
        _ada_kernel,
        out_shape=jax.ShapeDtypeStruct((depth, rows, n), F32),
        grid=(depth, n // tn),
        in_specs=[
            pl.BlockSpec((rows, d), lambda l, j: (0, 0)),
            pl.BlockSpec((1, d, tn), lambda l, j: (l, 0, j)),
            pl.BlockSpec((1, 1, tn), lambda l, j: (l, 0, j)),
        ],
        out_specs=pl.BlockSpec((1, rows, tn), lambda l, j: (l, 0, j)),
        compiler_params=_params(("parallel", "parallel"), 40),
        name="ada",
    )(cvec, ada_w, ada_b.reshape(depth, 1, n))


def _mla_proj_kernel(*refs, n_stream, q_scale):
    (mod_ref, ng_ref, wd_ref, gq_ref, gkv_ref, wqt_ref, wk_ref, wvt_ref,
     rc_ref, rs_ref, rct_ref, rst_ref, qt_ref, k_ref, vt_ref) = refs[n_stream:]
    x = _stream_tile(refs[:n_stream])
    mod = mod_ref[0]
    h = (_rms(x, ng_ref[0:1]) * (1.0 + mod[1:2]) + mod[0:1]).astype(BF16)
    a = _dot(h, wd_ref[...])
    nq, nkv = MLA_Q_RANK, MLA_KV_RANK
    cq = _rms(a[:, :nq], gq_ref[...]).astype(BF16)
    ckv = _rms(a[:, nq:nq + nkv], gkv_ref[...]).astype(BF16)
    o = nq + nkv
    k_rope = a[:, o:o + LANES] * rc_ref[...] + a[:, o + LANES:o + 2 * LANES] * rs_ref[...]
    k_rope = k_rope.astype(BF16)

    hw = MLA_HEADS * LANES
    qt = _dot_nt(wqt_ref[...], cq) * q_scale
    k_nope = _dot(ckv, wk_ref[...]).astype(BF16)
    vt = _dot_nt(wvt_ref[...], ckv).astype(BF16)
    rct = rct_ref[...]
    rst = rst_ref[...]
    for hd in range(MLA_HEADS):
        s = hd * LANES
        qt_ref[0, hd, 0:LANES, :] = qt[s:s + LANES].astype(BF16)
        q_rope = qt[hw + s:hw + s + LANES] * rct + qt[2 * hw + s:2 * hw + s + LANES] * rst
        qt_ref[0, hd, LANES:2 * LANES, :] = q_rope.astype(BF16)
        k_ref[0, hd, :, 0:LANES] = k_nope[:, s:s + LANES]
        k_ref[0, hd, :, LANES:2 * LANES] = k_rope
        vt_ref[0, hd, 0:MLA_V] = vt[s:s + LANES]
        vt_ref[0, hd, MLA_V:MLA_VX] = jnp.ones((MLA_VX - MLA_V, TILE), BF16)


def _mla_proj(stream, mod, ng, wd, gq, gkv, wqt, wk, wvt, rc, rs, rct, rst):
    b, _, d = stream[0].shape
    t = sum(a.shape[1] for a in stream)
    nt = t // TILE
    nb = b
    hds = MLA_HEADS
    q_scale = (MLA_NOPE + MLA_ROPE) ** -0.5 * LOG2E
    mod_idx = lambda bi, i: (jnp.where(i == nt - 1, nb, bi), 0, 0)
    return pl.pallas_call(
        functools.partial(_mla_proj_kernel, n_stream=len(stream), q_scale=q_scale),
        out_shape=(
            jax.ShapeDtypeStruct((b, hds, MLA_QK, t), BF16),
            jax.ShapeDtypeStruct((b, hds, t, MLA_QK), BF16),
            jax.ShapeDtypeStruct((b, hds, MLA_VX, t), BF16),
        ),
        grid=(b, nt),
        in_specs=_stream_specs(stream, nt, d) + [
            pl.BlockSpec((1, N_MOD, d), mod_idx),
            _const_spec(ng.shape),
            _const_spec(wd.shape),
            _const_spec(gq.shape),
            _const_spec(gkv.shape),
            _const_spec(wqt.shape),
            _const_spec(wk.shape),
            _const_spec(wvt.shape),
            pl.BlockSpec((TILE, LANES), lambda bi, i: (i, 0)),
            pl.BlockSpec((TILE, LANES), lambda bi, i: (i, 0)),
            pl.BlockSpec((LANES, TILE), lambda bi, i: (0, i)),
            pl.BlockSpec((LANES, TILE), lambda bi, i: (0, i)),
        ],
        out_specs=(
            pl.BlockSpec((1, hds, MLA_QK, TILE), lambda bi, i: (bi, 0, 0, i)),
            pl.BlockSpec((1, hds, TILE, MLA_QK), lambda bi, i: (bi, 0, i, 0)),
            pl.BlockSpec((1, hds, MLA_VX, TILE), lambda bi, i: (bi, 0, 0, i)),
        ),
        compiler_params=_params(("parallel", "parallel"), 48),
        name="mla_proj",
    )(*stream, mod, ng, wd, gq, gkv, wqt, wk, wvt, rc, rs, rct, rst)


def _attn_kernel(*refs, tq, tk, n_chunks, n_tiles, aliased):
    if aliased:
        qt_ref, k_ref, vt_ref, _, o_ref, s_ref, p_ref = refs
    else:
        qt_ref, k_ref, vt_ref, o_ref, s_ref, p_ref = refs
    assert n_chunks >= 2 or n_tiles == 1

    def s_slot(j):
        return 2 if j == 0 else j % 2

    def q_start(i):
        return i * tq if isinstance(i, int) else pl.multiple_of(i * tq, tq)

    def scores(i, j):
        qt = qt_ref[0, 0, :, pl.ds(q_start(i), tq)]
        k = k_ref[0, 0, j * tk:(j + 1) * tk, :]
        s = _dot(k, qt)
        s_ref[s_slot(j)] = s
        return jnp.max(s, axis=0, keepdims=True)

    def softmax(j, m, s_max):
        m_new = jnp.maximum(m, s_max)
        p = jnp.exp2(s_ref[s_slot(j)] - m_new)
        p_ref[j % 2] = p.astype(BF16)
        return m_new, jnp.exp2(m - m_new)

    def pv(j, acc, alpha):
        vt = vt_ref[0, 0, :, j * tk:(j + 1) * tk]
        return alpha * acc + _dot(vt, p_ref[j % 2])

    def tile(i, s_max):
        m = jnp.full((1, tq), -jnp.inf, F32)
        acc = jnp.zeros((MLA_VX, tq), F32)
        alpha = None
        for j in range(n_chunks):
            if j > 0:
                acc = pv(j - 1, acc, alpha)
            if j + 1 < n_chunks:
                next_max = scores(i, j + 1)
            elif n_tiles > 1:
                next_max = scores(jnp.minimum(i + 1, n_tiles - 1), 0)
            else:
                next_max = s_max
            m, alpha = softmax(j, m, s_max)
            s_max = next_max
        acc = pv(n_chunks - 1, acc, alpha)
        o = acc[:MLA_V] * (1.0 / acc[MLA_V:MLA_V + 1])
        o_ref[0, pl.ds(q_start(i), tq), :] = o.T.astype(o_ref.dtype)
        return s_max

    first_max = scores(0, 0)
    if n_tiles == 1:
        tile(0, first_max)
    else:
        lax.fori_loop(0, n_tiles, tile, first_max)


def _attn_tk(n_keys):
    for tk in (768, 512, 256):
        if n_keys % tk == 0:
            return tk
    raise ValueError(f"unsupported key count {n_keys}")


def _mla_attention(qt, k, vt, n_lat, tq):
    b, hds, _, t = qt.shape
    out_shape = jax.ShapeDtypeStruct((b, t, hds * MLA_V), BF16)
    sem = ("parallel", "parallel")
    tk = _attn_tk(t)
    o = pl.pallas_call(
        functools.partial(_attn_kernel, tq=tq, tk=tk, n_chunks=t // tk, n_tiles=n_lat // tq, aliased=False),
        out_shape=out_shape,
        grid=(b, hds),
        in_specs=[
            pl.BlockSpec((1, 1, MLA_QK, n_lat), lambda bi, h: (bi, h, 0, 0)),
            pl.BlockSpec((1, 1, t, MLA_QK), lambda bi, h: (bi, h, 0, 0)),
            pl.BlockSpec((1, 1, MLA_VX, t), lambda bi, h: (bi, h, 0, 0)),
        ],
        out_specs=pl.BlockSpec((1, n_lat, MLA_V), lambda bi, h: (bi, 0, h)),
        scratch_shapes=[pltpu.VMEM((3, tk, tq), F32), pltpu.VMEM((2, tk, tq), BF16)],
        compiler_params=_params(sem, 56),
        name="mla_attn_lat",
    )(qt, k, vt)
    ct = n_lat // TILE
    return pl.pallas_call(
        functools.partial(_attn_kernel, tq=TILE, tk=TILE, n_chunks=1, n_tiles=1, aliased=True),
        out_shape=out_shape,
        grid=(b, hds),
        in_specs=[
            pl.BlockSpec((1, 1, MLA_QK, TILE), lambda bi, h: (bi, h, 0, ct)),
            pl.BlockSpec((1, 1, TILE, MLA_QK), lambda bi, h: (bi, h, ct, 0)),
            pl.BlockSpec((1, 1, MLA_VX, TILE), lambda bi, h: (bi, h, 0, ct)),
            pl.BlockSpec(memory_space=pl.ANY),
        ],
        out_specs=pl.BlockSpec((1, TILE, MLA_V), lambda bi, h: (bi, ct, h)),
        input_output_aliases={3: 0},
        scratch_shapes=[pltpu.VMEM((3, TILE, TILE), F32), pltpu.VMEM((2, TILE, TILE), BF16)],
        compiler_params=_params(sem, 32),
        name="mla_attn_ctx",
    )(qt, k, vt, o)


def _gla_proj_kernel(x_ref, mod_ref, ng_ref, wg_ref, wvt_ref, w2_ref, bg_ref,
                     qf_ref, kf_ref, kfe_ref, qb_ref, kb_ref, kbe_ref, v_ref, vt_ref, r_ref, d_ref):
    x = x_ref[0]
    mod = mod_ref[0]
    h = (_rms(x, ng_ref[0:1]) * (1.0 + mod[1:2]) + mod[0:1]).astype(BF16)
    big = _dot(h, wg_ref[...])
    dk = GLA_HEADS * LANES
    dv = x.shape[1]
    q = big[:, :dk] * (LANES ** -0.5)
    k = big[:, dk:2 * dk]
    v_ref[0] = big[:, 2 * dk:2 * dk + dv].astype(BF16)
    rr = big[:, 2 * dk + dv:2 * dk + 2 * dv]
    r_ref[0] = (rr / (1.0 + jnp.exp(-rr))).astype(BF16)
    vt_ref[0] = _dot_nt(wvt_ref[...], h).astype(BF16)

    low = big[:, 2 * dk + 2 * dv:].astype(BF16)
    z = _dot(low, w2_ref[...]) + bg_ref[...]
    g = -(jnp.maximum(-z, 0.0) + jnp.log(1.0 + jnp.exp(-jnp.abs(z)))) * (1.0 / GLA_TAU)
    g_hi = g.astype(BF16)
    g_lo = (g - g_hi.astype(F32)).astype(BF16)

    row = lax.broadcasted_iota(jnp.int32, (TILE, TILE), 0)
    col = lax.broadcasted_iota(jnp.int32, (TILE, TILE), 1)
    shift = GLA_CHUNK.bit_length() - 1
    same = (row >> shift) == (col >> shift)
    prefix = jnp.where(same & (col <= row), 1.0, 0.0).astype(BF16)
    suffix = jnp.where(same & (col >= row), 1.0, 0.0).astype(BF16)

    def cums(mat, lo, hi):
        return _dot(mat, g_hi[:, lo:hi]) + _dot(mat, g_lo[:, lo:hi])

    gf = cums(prefix, 0, dk)
    gb = cums(suffix, dk, 2 * dk)
    n_chunks = TILE // GLA_CHUNK
    tot_f = [gf[(c + 1) * GLA_CHUNK - 1:(c + 1) * GLA_CHUNK] for c in range(n_chunks)]
    tot_b = [gb[c * GLA_CHUNK:c * GLA_CHUNK + 1] for c in range(n_chunks)]
    tf = jnp.concatenate([jnp.broadcast_to(t, (GLA_CHUNK, dk)) for t in tot_f], axis=0)
    tb = jnp.concatenate([jnp.broadcast_to(t, (GLA_CHUNK, dk)) for t in tot_b], axis=0)
    for c in range(n_chunks):
        d_ref[0, 0, c:c + 1, 0:dk] = jnp.exp(tot_f[c])
        d_ref[0, 0, c:c + 1, dk:2 * dk] = jnp.exp(tot_b[c])
    d_ref[0, 0, n_chunks:, :] = jnp.ones((d_ref.shape[2] - n_chunks, 2 * dk), F32)

    qf_ref[0] = (q * jnp.exp(gf)).astype(BF16)
    kf_ref[0] = (k * jnp.exp(-gf)).astype(BF16)
    kfe_ref[0] = (k * jnp.exp(tf - gf)).astype(BF16)
    qb_ref[0] = (q * jnp.exp(gb)).astype(BF16)
    kb_ref[0] = (k * jnp.exp(-gb)).astype(BF16)
    kbe_ref[0] = (k * jnp.exp(tb - gb)).astype(BF16)


def _gla_proj(xa, mod, ng, wg, wvt, w2, bg):
    b, t, d = xa.shape
    nt = t // TILE
    nb = b
    dk = GLA_HEADS * LANES
    mod_idx = lambda bi, i: (jnp.where(i == nt - 1, nb, bi), 0, 0)
    row_spec = lambda w: pl.BlockSpec((1, TILE, w), lambda bi, i: (bi, i, 0))
    qk = jax.ShapeDtypeStruct((b, t, dk), BF16)
    wide = jax.ShapeDtypeStruct((b, t, d), BF16)
    return pl.pallas_call(
        _gla_proj_kernel,
        out_shape=(qk, qk, qk, qk, qk, qk, wide,
                   jax.ShapeDtypeStruct((b, d, t), BF16), wide,
                   jax.ShapeDtypeStruct((b, nt, GLA_DECAY_ROWS, 2 * dk), F32)),
        grid=(b, nt),
        in_specs=[
            row_spec(d),
            pl.BlockSpec((1, N_MOD, d), mod_idx),
            _const_spec(ng.shape),
            _const_spec(wg.shape),
            _const_spec(wvt.shape),
            _const_spec(w2.shape),
            _const_spec(bg.shape),
        ],
        out_specs=(row_spec(dk),) * 6 + (
            row_spec(d),
            pl.BlockSpec((1, d, TILE), lambda bi, i: (bi, 0, i)),
            row_spec(d),
            pl.BlockSpec((1, 1, GLA_DECAY_ROWS, 2 * dk), lambda bi, i: (bi, i, 0, 0)),
        ),
        compiler_params=_params(("parallel", "parallel"), 48),
        name="gla_proj",
    )(xa, mod, ng, wg, wvt, w2, bg)


def _gla_scan_kernel(qf_ref, kf_ref, kfe_ref, vf_ref, vtf_ref, df_ref,
                     qb_ref, kb_ref, kbe_ref, vb_ref, vtb_ref, db_ref,
                     of_ref, ob_ref, s_ref):
    @pl.when(pl.program_id(1) == 0)
    def _():
        s_ref[...] = jnp.zeros(s_ref.shape, s_ref.dtype)

    L = GLA_CHUNK
    n_chunks = TILE // L
    dk = LANES
    dv = vf_ref.shape[2] // GLA_HEADS
    ri = lax.broadcasted_iota(jnp.int32, (L, L), 0)
    ci = lax.broadcasted_iota(jnp.int32, (L, L), 1)
    causal = ci <= ri
    anti = ci >= ri

    def run(direction, q_ref, k_ref, ke_ref, v_ref, vt_ref, d_ref, o_ref, mask, order):
        for c in order:
            rows = slice(c * L, (c + 1) * L)
            for hd in range(GLA_HEADS):
                kc = slice(hd * dk, (hd + 1) * dk)
                vc = slice(hd * dv, (hd + 1) * dv)
                q = q_ref[0, rows, kc]
                a = _dot_nt(q, k_ref[0, rows, kc])
                a = jnp.where(mask, a, 0.0).astype(BF16)
                idx = direction * GLA_HEADS + hd
                st = s_ref[idx]
                o = _dot(a, v_ref[0, rows, vc]) + _dot_nt(q, st.astype(BF16))
                o_ref[0, rows, vc] = o.astype(o_ref.dtype)
                decay = d_ref[0, 0, c:c + 1, direction * GLA_HEADS * dk + hd * dk:
                              direction * GLA_HEADS * dk + (hd + 1) * dk]
                s_ref[idx] = st * decay + _dot(vt_ref[0, vc, rows], ke_ref[0, rows, kc])

    run(0, qf_ref, kf_ref, kfe_ref, vf_ref, vtf_ref, df_ref, of_ref, causal, range(n_chunks))
    run(1, qb_ref, kb_ref, kbe_ref, vb_ref, vtb_ref, db_ref, ob_ref, anti, range(n_chunks - 1, -1, -1))


def _gla_scan(qf, kf, kfe, qb, kb, kbe, v, vt, dd):
    b, t, dkh = qf.shape
    d = v.shape[2]
    nt = t // TILE
    f_tile = lambda s: jnp.where(s == 0, nt - 1, s - 1)
    b_tile = lambda s: jnp.where(s == 0, nt - 1, nt - 1 - s)

    def specs(tile):
        row = lambda w: pl.BlockSpec((1, TILE, w), lambda bi, s: (bi, tile(s), 0))
        return [row(dkh), row(dkh), row(dkh), row(d),
                pl.BlockSpec((1, d, TILE), lambda bi, s: (bi, 0, tile(s))),
                pl.BlockSpec((1, 1, GLA_DECAY_ROWS, 2 * dkh), lambda bi, s: (bi, tile(s), 0, 0))]

    out = jax.ShapeDtypeStruct((b, t, d), BF16)
    return pl.pallas_call(
        _gla_scan_kernel,
        out_shape=(out, out),
        grid=(b, nt),
        in_specs=specs(f_tile) + specs(b_tile),
        out_specs=(pl.BlockSpec((1, TILE, d), lambda bi, s: (bi, f_tile(s), 0)),
                   pl.BlockSpec((1, TILE, d), lambda bi, s: (bi, b_tile(s), 0))),
        scratch_shapes=[pltpu.VMEM((2 * GLA_HEADS, d // GLA_HEADS, LANES), F32)],
        compiler_params=_params(("parallel", "arbitrary"), 32),
        name="gla_scan",
    )(qf, kf, kfe, v, vt, dd, qb, kb, kbe, v, vt, dd)


def _post_kernel(*refs, n_stream, gla):
    x = _stream_tile(refs[:n_stream])
    refs = refs[n_stream:]
    if gla:
        mod_ref, ng_ref, of_ref, ob_ref, r_ref, go_ref, wo_ref, w1_ref, w2_ref, out_ref = refs
        o = of_ref[0].astype(F32) + ob_ref[0].astype(F32)
        dv = o.shape[1] // GLA_HEADS
        go = go_ref[...]
        o = jnp.concatenate([_rms(o[:, i * dv:(i + 1) * dv], go) for i in range(GLA_HEADS)], axis=1)
        y_in = (o * r_ref[0].astype(F32)).astype(BF16)
    else:
        mod_ref, ng_ref, o_ref, wo_ref, w1_ref, w2_ref, out_ref = refs
        y_in = o_ref[0]
    mod = mod_ref[0]
    ng = ng_ref[...]
    y = _dot(y_in, wo_ref[...])
    x1 = x + mod[2:3] * _rms(y, ng[1:2])
    h = (_rms(x1, ng[2:3]) * (1.0 + mod[4:5]) + mod[3:4]).astype(BF16)
    d = x.shape[1]
    z = jnp.zeros(x.shape, F32)
    for j in range(w1_ref.shape[1] // d):
        u = jnp.maximum(_dot(h, w1_ref[:, j * d:(j + 1) * d]), 0.0)
        z = z + _dot((u * u).astype(BF16), w2_ref[j * d:(j + 1) * d, :])
    out_ref[0] = x1 + mod[5:6] * _rms(z, ng[3:4])


def _post(stream, mod, ng, mixer_out, weights, gla, latent_only):
    b, _, d = stream[0].shape
    t = mixer_out[0].shape[1]
    nt = t // TILE
    nb = b
    if latent_only:
        stream = stream[:1]
    n_out = nt - 1 if latent_only else nt
    mod_idx = lambda bi, i: (jnp.where(i == nt - 1, nb, bi), 0, 0)
    row = pl.BlockSpec((1, TILE, d), lambda bi, i: (bi, i, 0))
    in_place = len(stream) == 1 and not latent_only
    return pl.pallas_call(
        functools.partial(_post_kernel, n_stream=len(stream), gla=gla),
        out_shape=jax.ShapeDtypeStruct((b, n_out * TILE, d), stream[0].dtype),
        grid=(b, n_out),
        in_specs=_stream_specs(stream, nt, d)
        + [pl.BlockSpec((1, N_MOD, d), mod_idx), _const_spec(ng.shape)]
        + [row] * len(mixer_out) + [_const_spec(w.shape) for w in weights],
        out_specs=row,
        input_output_aliases={0: 0} if in_place else {},
        compiler_params=_params(("parallel", "parallel"), 56),
        name="gla_post" if gla else "mla_post",
    )(*stream, mod, ng, *mixer_out, *weights)


def _rope_tables(n_lat, n_ctx):
    t = jnp.arange(n_lat)
    row = (t // GRID_W).astype(F32)
    col = (t % GRID_W).astype(F32)
    half = MLA_ROPE // 2
    inv_freq = ROPE_BASE ** (-jnp.arange(0, half, 2, dtype=F32) / half)
    ang_r = row[:, None] * inv_freq
    ang_c = col[:, None] * inv_freq
    cr, sr, cc, sc = jnp.cos(ang_r), jnp.sin(ang_r), jnp.cos(ang_c), jnp.sin(ang_c)
    cos = jnp.concatenate([cr, cr, cc, cc], axis=1)
    sin = jnp.concatenate([-sr, sr, -sc, sc], axis=1)
    cos = jnp.concatenate([cos, jnp.ones((n_ctx, MLA_ROPE), F32)], axis=0)
    sin = jnp.concatenate([sin, jnp.zeros((n_ctx, MLA_ROPE), F32)], axis=0)
    pad = ((0, 0), (0, LANES - MLA_ROPE))
    return jnp.pad(cos, pad), jnp.pad(sin, pad)


def _swap_perm():
    q = MLA_ROPE // 4
    return jnp.array(list(range(q, 2 * q)) + list(range(0, q)) + list(range(3 * q, 4 * q)) + list(range(2 * q, 3 * q)))


def _mla_weights(w_dq, g_q, w_uq, w_dkv, g_kv, w_ukv, w_o):
    d = w_dq.shape[0]
    perm = _swap_perm()
    rope = w_dkv[:, MLA_KV_RANK:]
    zpad = jnp.zeros((d, LANES - MLA_ROPE), w_dkv.dtype)
    wd = jnp.concatenate([w_dq, w_dkv[:, :MLA_KV_RANK], rope, zpad, rope[:, perm], zpad], axis=1)
    uq = w_uq.reshape(MLA_Q_RANK, MLA_HEADS, MLA_NOPE + MLA_ROPE)
    q_nope = uq[:, :, :MLA_NOPE].reshape(MLA_Q_RANK, -1)
    q_rope = uq[:, :, MLA_NOPE:]
    hpad = ((0, 0), (0, 0), (0, LANES - MLA_ROPE))
    q_rope_p = jnp.pad(q_rope, hpad).reshape(MLA_Q_RANK, -1)
    q_swap_p = jnp.pad(q_rope[:, :, perm], hpad).reshape(MLA_Q_RANK, -1)
    wqt = jnp.concatenate([q_nope, q_rope_p, q_swap_p], axis=1).T
    ukv = w_ukv.reshape(MLA_KV_RANK, MLA_HEADS, MLA_NOPE + MLA_V)
    wk = ukv[:, :, :MLA_NOPE].reshape(MLA_KV_RANK, -1)
    wvt = ukv[:, :, MLA_NOPE:].reshape(MLA_KV_RANK, -1).T
    return (wd.astype(BF16), g_q.reshape(1, -1), g_kv.reshape(1, -1),
            wqt.astype(BF16), wk.astype(BF16), wvt.astype(BF16), w_o.astype(BF16))


def _gla_weights(w_q, w_k, w_v, w_r, w_gate1, w_gate2, b_gate, g_o, w_o):
    d = w_q.shape[0]
    dk = w_q.shape[1]
    low_pad = jnp.zeros((d, LANES - 2 * GLA_RANK), w_q.dtype)
    wg = jnp.concatenate([w_q, w_k, w_v, w_r, w_gate1[0], w_gate1[1], low_pad], axis=1)
    zero = jnp.zeros((GLA_RANK, dk), w_gate2.dtype)
    w2 = jnp.concatenate([
        jnp.concatenate([w_gate2[0], zero], axis=1),
        jnp.concatenate([zero, w_gate2[1]], axis=1),
        jnp.zeros((LANES - 2 * GLA_RANK, 2 * dk), w_gate2.dtype)], axis=0)
    bg = jnp.concatenate([b_gate[0], b_gate[1]]).reshape(1, -1)
    return (wg.astype(BF16), w_v.T.astype(BF16), w2.astype(BF16), bg,
            g_o.reshape(1, -1), w_o.astype(BF16))


def kernel(x, c, ctx, c_ctx, ada_w, ada_b, norm_g, mlp_w1, mlp_w2, mla_w_dq, mla_g_q, mla_w_uq, mla_w_dkv, mla_g_kv, mla_w_ukv, mla_w_o, gla_w_q, gla_w_k, gla_w_v, gla_w_r, gla_w_gate1, gla_w_gate2, gla_b_gate, gla_g_o, gla_w_o):
    b, s, d = x.shape
    n_ctx = ctx.shape[1]
    depth = ada_w.shape[0]
    assert n_ctx == TILE and s % TILE == 0 and d == GLA_HEADS * 2 * LANES

    stream = (x, ctx)
    rows = -(-(b + 1) // 16) * 16
    cvec = jnp.concatenate([c, c_ctx[None], jnp.zeros((rows - b - 1, d), c.dtype)], axis=0)
    mod_all = _ada(cvec, ada_w, ada_b).reshape(depth, rows, N_MOD, d)

    rc, rs = _rope_tables(s, n_ctx)
    rct, rst = rc.T, rs.T
    tq = 1024 if s % 1024 == 0 else TILE

    for i in range(depth):
        j = i // 2
        mod = mod_all[i]
        ng = norm_g[i]
        w1 = mlp_w1[i].astype(BF16)
        w2 = mlp_w2[i].astype(BF16)
        last = i == depth - 1
        if i % 2 == 0:
            wd, gq, gkv, wqt, wk, wvt, wo = _mla_weights(
                mla_w_dq[j], mla_g_q[j], mla_w_uq[j], mla_w_dkv[j], mla_g_kv[j], mla_w_ukv[j], mla_w_o[j])
            qt, k, vt = _mla_proj(stream, mod, ng, wd, gq, gkv, wqt, wk, wvt, rc, rs, rct, rst)
            o = _mla_attention(qt, k, vt, s, tq)
            stream = (_post(stream, mod, ng, (o,), (wo, w1, w2), gla=False, latent_only=last),)
        else:
            wg, wvt, w2g, bg, go, wo = _gla_weights(
                gla_w_q[j], gla_w_k[j], gla_w_v[j], gla_w_r[j], gla_w_gate1[j], gla_w_gate2[j],
                gla_b_gate[j], gla_g_o[j], gla_w_o[j])
            (xa,) = stream
            qf, kf, kfe, qb, kb, kbe, v, vt, r, dd = _gla_proj(xa, mod, ng, wg, wvt, w2g, bg)
            of, ob = _gla_scan(qf, kf, kfe, qb, kb, kbe, v, vt, dd)
            stream = (_post(stream, mod, ng, (of, ob, r), (go, wo, w1, w2), gla=True, latent_only=last),)
    return stream[0]
```

```python
import functools
import math

import jax
import jax.numpy as jnp
from jax import lax
from jax.experimental import pallas as pl
from jax.experimental.pallas import tpu as pltpu

N_MOD = 6
NORM_EPS = 1e-6
GRID_W = 64
MLA_HEADS = 8
MLA_NOPE = 128
MLA_ROPE = 64
MLA_V = 128
MLA_VX = MLA_V + 16
MLA_Q_RANK = 384
MLA_KV_RANK = 256
ROPE_BASE = 10000.0
GLA_HEADS = 4
GLA_RANK = 16
GLA_TAU = 16.0
GLA_CHUNK = 64

LANES = 128
TILE = 256
MLA_QK = 2 * LANES
GLA_DECAY_ROWS = 8
MIB = 1024 * 1024
LOG2E = 1.4426950408889634

F32 = jnp.float32
BF16 = jnp.bfloat16
NT = (((1,), (1,)), ((), ()))


def _dot(a, b):
    return jnp.dot(a, b, preferred_element_type=F32)


def _dot_nt(a, b):
    return lax.dot_general(a, b, NT, preferred_element_type=F32)


def _rms(x, g):
    ms = jnp.mean(x * x, axis=-1, keepdims=True)
    return x * lax.rsqrt(ms + NORM_EPS) * g


def _const_spec(shape):
    nd = len(shape)
    return pl.BlockSpec(shape, lambda *_: (0,) * nd, pipeline_mode=pl.Buffered(1))


def _params(sem, vmem_mib):
    return pltpu.CompilerParams(dimension_semantics=sem, vmem_limit_bytes=vmem_mib * MIB)


def _stream_specs(stream, n_tiles, d, nb=1):
    if len(stream) == 1:
        return [pl.BlockSpec((nb, TILE, d), lambda bi, i: (bi, i, 0))]
    last_lat = n_tiles - 2
    return [pl.BlockSpec((nb, TILE, d), lambda bi, i: (bi, jnp.minimum(i, last_lat), 0)),
            pl.BlockSpec((nb, TILE, d), lambda bi, i: (bi, 0, 0))]


def _stream_tile(stream_refs, u=0):
    if len(stream_refs) == 1:
        return stream_refs[0][u]
    is_ctx = pl.program_id(1) == pl.num_programs(1) - 1
    return jnp.where(is_ctx, stream_refs[1][u], stream_refs[0][u])


def _batch_group(b):
    return 2 if b % 2 == 0 else 1


def _ada_kernel(c_ref, w_ref, b_ref, o_ref):
    c = c_ref[...]
    a = c / (1.0 + jnp.exp(-c))
    a_hi = a.astype(BF16)
    a_lo = (a - a_hi.astype(F32)).astype(BF16)
    w = w_ref[0]
    w_hi = w.astype(BF16)
    w_lo = (w - w_hi.astype(F32)).astype(BF16)
    o_ref[0] = _dot(a_hi, w_hi) + _dot(a_lo, w_hi) + _dot(a_hi, w_lo) + b_ref[0]


def _ada(cvec, ada_w, ada_b):
    depth, d, n = ada_w.shape
    rows = cvec.shape[0]
    tn = 1536
    return pl.pallas_call(
        _ada_kernel,
        out_shape=jax.ShapeDtypeStruct((depth, rows, n), F32),
        grid=(depth, n // tn),
        in_specs=[
            pl.BlockSpec((rows, d), lambda l, j: (0, 0)),
            pl.BlockSpec((1, d, tn), lambda l, j: (l, 0, j)),
            pl.BlockSpec((1, 1, tn), lambda l, j: (l, 0, j)),
        ],
        out_specs=pl.BlockSpec((1, rows, tn), lambda l, j: (l, 0, j)),
        compiler_params=_params(("parallel", "parallel"), 40),
        name="ada",
    )(cvec, ada_w, ada_b.reshape(depth, 1, n))


def _mla_proj_kernel(*refs, n_stream, q_scale):
    (mod_ref, ng_ref, wd_ref, gq_ref, gkv_ref, wqt_ref, wk_ref, wvt_ref,
     rcs_ref, rcst_ref, qt_ref, k_ref, vt_ref) = refs[n_stream:]
    stream_refs = refs[:n_stream]
    group = range(qt_ref.shape[0])
    nq, nkv, nr = MLA_Q_RANK, MLA_KV_RANK, MLA_ROPE
    hw = MLA_HEADS * LANES
    hr = MLA_HEADS * nr

    def down(u):
        mod = mod_ref[u]
        h = (_rms(_stream_tile(stream_refs, u), ng_ref[0:1]) * (1.0 + mod[1:2]) + mod[0:1]).astype(BF16)
        return _dot(h, wd_ref[...])

    def latents(a):
        cq = _rms(a[:, :nq], gq_ref[...]).astype(BF16)
        ckv = _rms(a[:, nq:nq + nkv], gkv_ref[...]).astype(BF16)
        y = a[:, nq + nkv:] * rcs_ref[...]
        lane = lax.broadcasted_iota(jnp.int32, y.shape, 1)
        k_rope = jnp.where(lane < nr, y + pltpu.roll(y, nr, axis=1), 0.0).astype(BF16)
        return cq, ckv, k_rope

    def up(cq, ckv):
        qt = _dot_nt(wqt_ref[...], cq) * q_scale
        k_nope = _dot(ckv, wk_ref[...]).astype(BF16)
        vt = _dot_nt(wvt_ref[...], ckv).astype(BF16)
        return qt, k_nope, vt

    def emit(u, k_rope, qt, k_nope, vt):
        cos_t = rcst_ref[0:nr]
        sin_t = rcst_ref[nr:2 * nr]
        for hd in range(MLA_HEADS):
            s = hd * LANES
            r = hw + hd * nr
            qt_ref[u, hd, 0:LANES, :] = qt[s:s + LANES].astype(BF16)
            q_rope = qt[r:r + nr] * cos_t + qt[r + hr:r + hr + nr] * sin_t
            qt_ref[u, hd, LANES:LANES + nr, :] = q_rope.astype(BF16)
            qt_ref[u, hd, LANES + nr:, :] = jnp.zeros((MLA_QK - LANES - nr, TILE), BF16)
            k_ref[u, hd, :, 0:LANES] = k_nope[:, s:s + LANES]
            k_ref[u, hd, :, LANES:2 * LANES] = k_rope
            vt_ref[u, hd, 0:MLA_V] = vt[s:s + LANES]
            vt_ref[u, hd, MLA_V:MLA_VX] = jnp.ones((MLA_VX - MLA_V, TILE), BF16)

    downs = [down(u) for u in group]
    lats = [latents(downs[u]) for u in group]
    ups = [up(lats[u][0], lats[u][1]) for u in group]
    for u in group:
        emit(u, lats[u][2], *ups[u])


def _mla_proj(stream, mod, ng, wd, gq, gkv, wqt, wk, wvt, rcs, rcst):
    b, _, d = stream[0].shape
    t = sum(a.shape[1] for a in stream)
    nt = t // TILE
    nb = b
    hds = MLA_HEADS
    g = _batch_group(b)
    q_scale = (MLA_NOPE + MLA_ROPE) ** -0.5 * LOG2E
    mod_idx = lambda bi, i: (jnp.where(i == nt - 1, nb // g, bi), 0, 0)
    return pl.pallas_call(
        functools.partial(_mla_proj_kernel, n_stream=len(stream), q_scale=q_scale),
        out_shape=(
            jax.ShapeDtypeStruct((b, hds, MLA_QK, t), BF16),
            jax.ShapeDtypeStruct((b, hds, t, MLA_QK), BF16),
            jax.ShapeDtypeStruct((b, hds, MLA_VX, t), BF16),
        ),
        grid=(b // g, nt),
        in_specs=_stream_specs(stream, nt, d, g) + [
            pl.BlockSpec((g, N_MOD, d), mod_idx),
            _const_spec(ng.shape),
            _const_spec(wd.shape),
            _const_spec(gq.shape),
            _const_spec(gkv.shape),
            _const_spec(wqt.shape),
            _const_spec(wk.shape),
            _const_spec(wvt.shape),
            pl.BlockSpec((TILE, LANES), lambda bi, i: (i, 0)),
            pl.BlockSpec((LANES, TILE), lambda bi, i: (0, i)),
        ],
        out_specs=(
            pl.BlockSpec((g, hds, MLA_QK, TILE), lambda bi, i: (bi, 0, 0, i)),
            pl.BlockSpec((g, hds, TILE, MLA_QK), lambda bi, i: (bi, 0, i, 0)),
            pl.BlockSpec((g, hds, MLA_VX, TILE), lambda bi, i: (bi, 0, 0, i)),
        ),
        compiler_params=_params(("parallel", "parallel"), 48),
        name="mla_proj",
    )(*stream, mod, ng, wd, gq, gkv, wqt, wk, wvt, rcs, rcst)


def _attn_kernel(*refs, tq, tk, n_chunks, n_tiles, aliased):
    if aliased:
        qt_ref, k_ref, vt_ref, _, o_ref, s_ref, p_ref = refs
    else:
        qt_ref, k_ref, vt_ref, o_ref, s_ref, p_ref = refs
    assert n_chunks >= 2 or n_tiles == 1

    def s_slot(j):
        return 2 if j == 0 else j % 2

    def q_start(i):
        return i * tq if isinstance(i, int) else pl.multiple_of(i * tq, tq)

    def scores(i, j):
        qt = qt_ref[0, 0, :, pl.ds(q_start(i), tq)]
        k = k_ref[0, 0, j * tk:(j + 1) * tk, :]
        s = _dot(k, qt)
        s_ref[s_slot(j)] = s
        return jnp.max(s, axis=0, keepdims=True)

    def softmax(j, m, s_max):
        m_new = jnp.maximum(m, s_max)
        p = jnp.exp2(s_ref[s_slot(j)] - m_new)
        p_ref[j % 2] = p.astype(BF16)
        return m_new, jnp.exp2(m - m_new)

    def pv(j, acc, alpha):
        vt = vt_ref[0, 0, :, j * tk:(j + 1) * tk]
        return alpha * acc + _dot(vt, p_ref[j % 2])

    def tile(i, s_max):
        m = jnp.full((1, tq), -jnp.inf, F32)
        acc = jnp.zeros((MLA_VX, tq), F32)
        alpha = None
        for j in range(n_chunks):
            if j > 0:
                acc = pv(j - 1, acc, alpha)
            if j + 1 < n_chunks:
                next_max = scores(i, j + 1)
            elif n_tiles > 1:
                next_max = scores(jnp.minimum(i + 1, n_tiles - 1), 0)
            else:
                next_max = s_max
            m, alpha = softmax(j, m, s_max)
            s_max = next_max
        acc = pv(n_chunks - 1, acc, alpha)
        o = acc[:MLA_V] * (1.0 / acc[MLA_V:MLA_V + 1])
        o_ref[0, pl.ds(q_start(i), tq), :] = o.T.astype(o_ref.dtype)
        return s_max

    first_max = scores(0, 0)
    if n_tiles == 1:
        tile(0, first_max)
    else:
        lax.fori_loop(0, n_tiles, tile, first_max)


def _attn_tk(n_keys):
    for tk in (768, 512, 256):
        if n_keys % tk == 0:
            return tk
    raise ValueError(f"unsupported key count {n_keys}")


def _mla_attention(qt, k, vt, n_lat, tq):
    b, hds, _, t = qt.shape
    out_shape = jax.ShapeDtypeStruct((b, t, hds * MLA_V), BF16)
    sem = ("parallel", "parallel")
    tk = _attn_tk(t)
    o = pl.pallas_call(
        functools.partial(_attn_kernel, tq=tq, tk=tk, n_chunks=t // tk, n_tiles=n_lat // tq, aliased=False),
        out_shape=out_shape,
        grid=(b, hds),
        in_specs=[
            pl.BlockSpec((1, 1, MLA_QK, n_lat), lambda bi, h: (bi, h, 0, 0)),
            pl.BlockSpec((1, 1, t, MLA_QK), lambda bi, h: (bi, h, 0, 0)),
            pl.BlockSpec((1, 1, MLA_VX, t), lambda bi, h: (bi, h, 0, 0)),
        ],
        out_specs=pl.BlockSpec((1, n_lat, MLA_V), lambda bi, h: (bi, 0, h)),
        scratch_shapes=[pltpu.VMEM((3, tk, tq), F32), pltpu.VMEM((2, tk, tq), BF16)],
        compiler_params=_params(sem, 56),
        name="mla_attn_lat",
    )(qt, k, vt)
    ct = n_lat // TILE
    return pl.pallas_call(
        functools.partial(_attn_kernel, tq=TILE, tk=TILE, n_chunks=1, n_tiles=1, aliased=True),
        out_shape=out_shape,
        grid=(b, hds),
        in_specs=[
            pl.BlockSpec((1, 1, MLA_QK, TILE), lambda bi, h: (bi, h, 0, ct)),
            pl.BlockSpec((1, 1, TILE, MLA_QK), lambda bi, h: (bi, h, ct, 0)),
            pl.BlockSpec((1, 1, MLA_VX, TILE), lambda bi, h: (bi, h, 0, ct)),
            pl.BlockSpec(memory_space=pl.ANY),
        ],
        out_specs=pl.BlockSpec((1, TILE, MLA_V), lambda bi, h: (bi, ct, h)),
        input_output_aliases={3: 0},
        scratch_shapes=[pltpu.VMEM((3, TILE, TILE), F32), pltpu.VMEM((2, TILE, TILE), BF16)],
        compiler_params=_params(sem, 32),
        name="mla_attn_ctx",
    )(qt, k, vt, o)


def _gla_proj_kernel(x_ref, mod_ref, ng_ref, wg_ref, w2_ref, bg_ref,
                     qf_ref, kf_ref, kfe_ref, qb_ref, kb_ref, kbe_ref, v_ref, r_ref, d_ref):
    group = range(x_ref.shape[0])
    dk = GLA_HEADS * LANES
    dv = x_ref.shape[2]
    n_chunks = TILE // GLA_CHUNK

    def project(u):
        mod = mod_ref[u]
        h = (_rms(x_ref[u], ng_ref[0:1]) * (1.0 + mod[1:2]) + mod[0:1]).astype(BF16)
        return _dot(h, wg_ref[...])

    def gates(u, big):
        v_ref[u] = big[:, 2 * dk:2 * dk + dv].astype(BF16)
        rr = big[:, 2 * dk + dv:2 * dk + 2 * dv]
        r_ref[u] = (rr / (1.0 + jnp.exp(-rr))).astype(BF16)
        low = big[:, 2 * dk + 2 * dv:].astype(BF16)
        z = _dot(low, w2_ref[...]) + bg_ref[...]
        g = -(jnp.maximum(-z, 0.0) + jnp.log(1.0 + jnp.exp(-jnp.abs(z)))) * (1.0 / GLA_TAU)
        g_hi = g.astype(BF16)
        return g_hi, (g - g_hi.astype(F32)).astype(BF16)

    row = lax.broadcasted_iota(jnp.int32, (TILE, TILE), 0)
    col = lax.broadcasted_iota(jnp.int32, (TILE, TILE), 1)
    shift = GLA_CHUNK.bit_length() - 1
    same = (row >> shift) == (col >> shift)
    prefix = jnp.where(same & (col <= row), 1.0, 0.0).astype(BF16)
    suffix = jnp.where(same & (col >= row), 1.0, 0.0).astype(BF16)

    def cumsums(g_hi, g_lo):
        gf = _dot(prefix, g_hi[:, :dk]) + _dot(prefix, g_lo[:, :dk])
        gb = _dot(suffix, g_hi[:, dk:]) + _dot(suffix, g_lo[:, dk:])
        return gf, gb

    def emit(u, big, gf, gb):
        q = big[:, :dk] * (LANES ** -0.5)
        k = big[:, dk:2 * dk]
        tot_f = [gf[(c + 1) * GLA_CHUNK - 1:(c + 1) * GLA_CHUNK] for c in range(n_chunks)]
        tot_b = [gb[c * GLA_CHUNK:c * GLA_CHUNK + 1] for c in range(n_chunks)]
        tf = jnp.concatenate([jnp.broadcast_to(t, (GLA_CHUNK, dk)) for t in tot_f], axis=0)
        tb = jnp.concatenate([jnp.broadcast_to(t, (GLA_CHUNK, dk)) for t in tot_b], axis=0)
        for c in range(n_chunks):
            d_ref[u, 0, c:c + 1, 0:dk] = jnp.exp(tot_f[c])
            d_ref[u, 0, c:c + 1, dk:2 * dk] = jnp.exp(tot_b[c])
        d_ref[u, 0, n_chunks:, :] = jnp.ones((d_ref.shape[2] - n_chunks, 2 * dk), F32)
        qf_ref[u] = (q * jnp.exp(gf)).astype(BF16)
        kf_ref[u] = (k * jnp.exp(-gf)).astype(BF16)
        kfe_ref[u] = (k * jnp.exp(tf - gf)).astype(BF16)
        qb_ref[u] = (q * jnp.exp(gb)).astype(BF16)
        kb_ref[u] = (k * jnp.exp(-gb)).astype(BF16)
        kbe_ref[u] = (k * jnp.exp(tb - gb)).astype(BF16)

    bigs = [project(u) for u in group]
    splits = [gates(u, bigs[u]) for u in group]
    sums = [cumsums(*splits[u]) for u in group]
    for u in group:
        emit(u, bigs[u], *sums[u])


def _gla_proj(xa, mod, ng, wg, w2, bg):
    b, t, d = xa.shape
    nt = t // TILE
    nb = b
    dk = GLA_HEADS * LANES
    g = _batch_group(b)
    mod_idx = lambda bi, i: (jnp.where(i == nt - 1, nb // g, bi), 0, 0)
    row_spec = lambda w: pl.BlockSpec((g, TILE, w), lambda bi, i: (bi, i, 0))
    qk = jax.ShapeDtypeStruct((b, t, dk), BF16)
    wide = jax.ShapeDtypeStruct((b, t, d), BF16)
    return pl.pallas_call(
        _gla_proj_kernel,
        out_shape=(qk, qk, qk, qk, qk, qk, wide, wide,
                   jax.ShapeDtypeStruct((b, nt, GLA_DECAY_ROWS, 2 * dk), F32)),
        grid=(b // g, nt),
        in_specs=[
            row_spec(d),
            pl.BlockSpec((g, N_MOD, d), mod_idx),
            _const_spec(ng.shape),
            _const_spec(wg.shape),
            _const_spec(w2.shape),
            _const_spec(bg.shape),
        ],
        out_specs=(row_spec(dk),) * 6 + (
            row_spec(d),
            row_spec(d),
            pl.BlockSpec((g, 1, GLA_DECAY_ROWS, 2 * dk), lambda bi, i: (bi, i, 0, 0)),
        ),
        compiler_params=_params(("parallel", "parallel"), 56),
        name="gla_proj",
    )(xa, mod, ng, wg, w2, bg)


def _gla_scan_kernel(qf_ref, kf_ref, kfe_ref, vf_ref, df_ref,
                     qb_ref, kb_ref, kbe_ref, vb_ref, db_ref,
                     of_ref, ob_ref, s_ref):
    @pl.when(pl.program_id(1) == 0)
    def _():
        s_ref[...] = jnp.zeros(s_ref.shape, s_ref.dtype)

    L = GLA_CHUNK
    n_chunks = TILE // L
    dk = LANES
    dv = vf_ref.shape[2] // GLA_HEADS
    shift = L.bit_length() - 1
    ri = lax.broadcasted_iota(jnp.int32, (TILE, TILE), 0)
    ci = lax.broadcasted_iota(jnp.int32, (TILE, TILE), 1)
    same_chunk = (ri >> shift) == (ci >> shift)
    causal = same_chunk & (ci <= ri)
    anti = same_chunk & (ci >= ri)
    token_chunk = lax.broadcasted_iota(jnp.int32, (dk, TILE), 1) >> shift

    def run(direction, q_ref, k_ref, ke_ref, v_ref, d_ref, o_ref, mask, order):
        for hd in range(GLA_HEADS):
            kc = slice(hd * dk, (hd + 1) * dk)
            vc = slice(hd * dv, (hd + 1) * dv)
            idx = direction * GLA_HEADS + hd
            q = q_ref[0, :, kc]
            v = v_ref[0, :, vc]
            a = jnp.where(mask, _dot_nt(q, k_ref[0, :, kc]), 0.0).astype(BF16)
            kt = ke_ref[0, :, kc].astype(F32).T
            kt_blocks = [jnp.where(token_chunk == c, kt, 0.0).astype(BF16) for c in range(n_chunks)]
            r = _dot(jnp.concatenate([a] + kt_blocks, axis=0), v)
            d_rows = d_ref[0, 0, :, idx * dk:(idx + 1) * dk]
            d_cols = jnp.concatenate([d_rows, jnp.zeros((dk - d_rows.shape[0], dk), F32)], axis=0).T
            st = s_ref[idx]
            for c in order:
                rows = slice(c * L, (c + 1) * L)
                o = r[rows] + _dot(q[rows], st.astype(BF16))
                o_ref[0, rows, vc] = o.astype(o_ref.dtype)
                st = st * d_cols[:, c:c + 1] + r[TILE + c * dk:TILE + (c + 1) * dk]
            s_ref[idx] = st

    run(0, qf_ref, kf_ref, kfe_ref, vf_ref, df_ref, of_ref, causal, range(n_chunks))
    run(1, qb_ref, kb_ref, kbe_ref, vb_ref, db_ref, ob_ref, anti, range(n_chunks - 1, -1, -1))


def _gla_scan(qf, kf, kfe, qb, kb, kbe, v, dd):
    b, t, dkh = qf.shape
    d = v.shape[2]
    nt = t // TILE
    f_tile = lambda s: jnp.where(s == 0, nt - 1, s - 1)
    b_tile = lambda s: jnp.where(s == 0, nt - 1, nt - 1 - s)

    def specs(tile):
        row = lambda w: pl.BlockSpec((1, TILE, w), lambda bi, s: (bi, tile(s), 0))
        return [row(dkh), row(dkh), row(dkh), row(d),
                pl.BlockSpec((1, 1, GLA_DECAY_ROWS, 2 * dkh), lambda bi, s: (bi, tile(s), 0, 0))]

    out = jax.ShapeDtypeStruct((b, t, d), BF16)
    return pl.pallas_call(
        _gla_scan_kernel,
        out_shape=(out, out),
        grid=(b, nt),
        in_specs=specs(f_tile) + specs(b_tile),
        out_specs=(pl.BlockSpec((1, TILE, d), lambda bi, s: (bi, f_tile(s), 0)),
                   pl.BlockSpec((1, TILE, d), lambda bi, s: (bi, b_tile(s), 0))),
        scratch_shapes=[pltpu.VMEM((2 * GLA_HEADS, LANES, d // GLA_HEADS), F32)],
        compiler_params=_params(("parallel", "arbitrary"), 32),
        name="gla_scan",
    )(qf, kf, kfe, v, dd, qb, kb, kbe, v, dd)


def _post_kernel(*refs, n_stream, gla):
    stream_refs = refs[:n_stream]
    refs = refs[n_stream:]
    if gla:
        mod_ref, ng_ref, of_ref, ob_ref, r_ref, go_ref, wo_ref, w1_ref, w2_ref, out_ref = refs
    else:
        mod_ref, ng_ref, o_ref, wo_ref, w1_ref, w2_ref, out_ref = refs
    ng = ng_ref[...]
    group = range(out_ref.shape[0])
    d = out_ref.shape[2]

    def mixer_proj(u):
        if gla:
            o = of_ref[u].astype(F32) + ob_ref[u].astype(F32)
            dv = d // GLA_HEADS
            go = go_ref[...]
            o = jnp.concatenate([_rms(o[:, i * dv:(i + 1) * dv], go) for i in range(GLA_HEADS)], axis=1)
            y_in = (o * r_ref[u].astype(F32)).astype(BF16)
        else:
            y_in = o_ref[u]
        return _dot(y_in, wo_ref[...])

    def residual(u, y):
        mod = mod_ref[u]
        x1 = _stream_tile(stream_refs, u) + mod[2:3] * _rms(y, ng[1:2])
        return x1, (_rms(x1, ng[2:3]) * (1.0 + mod[4:5]) + mod[3:4]).astype(BF16)

    def mlp(h):
        z = jnp.zeros((TILE, d), F32)
        for j in range(w1_ref.shape[1] // d):
            a = jnp.maximum(_dot(h, w1_ref[:, j * d:(j + 1) * d]), 0.0)
            z = z + _dot((a * a).astype(BF16), w2_ref[j * d:(j + 1) * d, :])
        return z

    ys = [mixer_proj(u) for u in group]
    zs, x1s = [], []
    for u in group:
        x1, h = residual(u, ys[u])
        x1s.append(x1)
        zs.append(mlp(h))
    for u in group:
        out_ref[u] = x1s[u] + mod_ref[u][5:6] * _rms(zs[u], ng[3:4])


def _post(stream, mod, ng, mixer_out, weights, gla, latent_only):
    b, _, d = stream[0].shape
    t = mixer_out[0].shape[1]
    nt = t // TILE
    nb = b
    if latent_only:
        stream = stream[:1]
    n_out = nt - 1 if latent_only else nt
    g = _batch_group(b)
    mod_idx = lambda bi, i: (jnp.where(i == nt - 1, nb // g, bi), 0, 0)
    row = pl.BlockSpec((g, TILE, d), lambda bi, i: (bi, i, 0))
    in_place = len(stream) == 1 and not latent_only
    return pl.pallas_call(
        functools.partial(_post_kernel, n_stream=len(stream), gla=gla),
        out_shape=jax.ShapeDtypeStruct((b, n_out * TILE, d), stream[0].dtype),
        grid=(b // g, n_out),
        in_specs=_stream_specs(stream, nt, d, g)
        + [pl.BlockSpec((g, N_MOD, d), mod_idx), _const_spec(ng.shape)]
        + [row] * len(mixer_out) + [_const_spec(w.shape) for w in weights],
        out_specs=row,
        input_output_aliases={0: 0} if in_place else {},
        compiler_params=_params(("parallel", "parallel"), 56),
        name="gla_post" if gla else "mla_post",
    )(*stream, mod, ng, *mixer_out, *weights)


def _rope_tables(n_lat, n_ctx):
    t = jnp.arange(n_lat)
    row = (t // GRID_W).astype(F32)
    col = (t % GRID_W).astype(F32)
    half = MLA_ROPE // 2
    inv_freq = ROPE_BASE ** (-jnp.arange(0, half, 2, dtype=F32) / half)
    ang_r = row[:, None] * inv_freq
    ang_c = col[:, None] * inv_freq
    cr, sr, cc, sc = jnp.cos(ang_r), jnp.sin(ang_r), jnp.cos(ang_c), jnp.sin(ang_c)
    cos = jnp.concatenate([cr, cr, cc, cc], axis=1)
    sin = jnp.concatenate([-sr, sr, -sc, sc], axis=1)
    cos = jnp.concatenate([cos, jnp.ones((n_ctx, MLA_ROPE), F32)], axis=0)
    sin = jnp.concatenate([sin, jnp.zeros((n_ctx, MLA_ROPE), F32)], axis=0)
    assert 2 * MLA_ROPE == LANES
    return jnp.concatenate([cos, sin], axis=1)


def _swap_perm():
    q = MLA_ROPE // 4
    return jnp.array(list(range(q, 2 * q)) + list(range(0, q)) + list(range(3 * q, 4 * q)) + list(range(2 * q, 3 * q)))


def _mla_weights(w_dq, g_q, w_uq, w_dkv, g_kv, w_ukv, w_o):
    perm = _swap_perm()
    rope = w_dkv[:, MLA_KV_RANK:]
    wd = jnp.concatenate([w_dq, w_dkv[:, :MLA_KV_RANK], rope, rope[:, perm]], axis=1)
    uq = w_uq.reshape(MLA_Q_RANK, MLA_HEADS, MLA_NOPE + MLA_ROPE)
    q_nope = uq[:, :, :MLA_NOPE].reshape(MLA_Q_RANK, -1)
    q_rope = uq[:, :, MLA_NOPE:]
    q_swap = q_rope[:, :, perm]
    wqt = jnp.concatenate([q_nope, q_rope.reshape(MLA_Q_RANK, -1), q_swap.reshape(MLA_Q_RANK, -1)], axis=1).T
    ukv = w_ukv.reshape(MLA_KV_RANK, MLA_HEADS, MLA_NOPE + MLA_V)
    wk = ukv[:, :, :MLA_NOPE].reshape(MLA_KV_RANK, -1)
    wvt = ukv[:, :, MLA_NOPE:].reshape(MLA_KV_RANK, -1).T
    return (wd.astype(BF16), g_q.reshape(1, -1), g_kv.reshape(1, -1),
            wqt.astype(BF16), wk.astype(BF16), wvt.astype(BF16), w_o.astype(BF16))


def _gla_weights(w_q, w_k, w_v, w_r, w_gate1, w_gate2, b_gate, g_o, w_o):
    d = w_q.shape[0]
    dk = w_q.shape[1]
    low_pad = jnp.zeros((d, LANES - 2 * GLA_RANK), w_q.dtype)
    wg = jnp.concatenate([w_q, w_k, w_v, w_r, w_gate1[0], w_gate1[1], low_pad], axis=1)
    zero = jnp.zeros((GLA_RANK, dk), w_gate2.dtype)
    w2 = jnp.concatenate([
        jnp.concatenate([w_gate2[0], zero], axis=1),
        jnp.concatenate([zero, w_gate2[1]], axis=1),
        jnp.zeros((LANES - 2 * GLA_RANK, 2 * dk), w_gate2.dtype)], axis=0)
    bg = jnp.concatenate([b_gate[0], b_gate[1]]).reshape(1, -1)
    return wg.astype(BF16), w2.astype(BF16), bg, g_o.reshape(1, -1), w_o.astype(BF16)


def kernel(x, c, ctx, c_ctx, ada_w, ada_b, norm_g, mlp_w1, mlp_w2, mla_w_dq, mla_g_q, mla_w_uq, mla_w_dkv, mla_g_kv, mla_w_ukv, mla_w_o, gla_w_q, gla_w_k, gla_w_v, gla_w_r, gla_w_gate1, gla_w_gate2, gla_b_gate, gla_g_o, gla_w_o):
    b, s, d = x.shape
    n_ctx = ctx.shape[1]
    depth = ada_w.shape[0]
    assert n_ctx == TILE and s % TILE == 0 and d == GLA_HEADS * 2 * LANES

    stream = (x, ctx)
    g = _batch_group(b)
    rows = -(-(b + g) // 16) * 16
    cvec = jnp.concatenate([c, jnp.tile(c_ctx[None], (g, 1)), jnp.zeros((rows - b - g, d), c.dtype)], axis=0)
    mod_all = _ada(cvec, ada_w, ada_b).reshape(depth, rows, N_MOD, d)

    rcs = _rope_tables(s, n_ctx)
    rcst = rcs.T
    tq = 1024 if s % 1024 == 0 else TILE

    for i in range(depth):
        j = i // 2
        mod = mod_all[i]
        ng = norm_g[i]
        w1 = mlp_w1[i].astype(BF16)
        w2 = mlp_w2[i].astype(BF16)
        last = i == depth - 1
        if i % 2 == 0:
            wd, gq, gkv, wqt, wk, wvt, wo = _mla_weights(
                mla_w_dq[j], mla_g_q[j], mla_w_uq[j], mla_w_dkv[j], mla_g_kv[j], mla_w_ukv[j], mla_w_o[j])
            qt, k, vt = _mla_proj(stream, mod, ng, wd, gq, gkv, wqt, wk, wvt, rcs, rcst)
            o = _mla_attention(qt, k, vt, s, tq)
            stream = (_post(stream, mod, ng, (o,), (wo, w1, w2), gla=False, latent_only=last),)
        else:
            wg, w2g, bg, go, wo = _gla_weights(
                gla_w_q[j], gla_w_k[j], gla_w_v[j], gla_w_r[j], gla_w_gate1[j], gla_w_gate2[j],
                gla_b_gate[j], gla_g_o[j], gla_w_o[j])
            (xa,) = stream
            qf, kf, kfe, qb, kb, kbe, v, r, dd = _gla_proj(xa, mod, ng, wg, w2g, bg)
            of, ob = _gla_scan(qf, kf, kfe, qb, kb, kbe, v, dd)
            stream = (_post(stream, mod, ng, (of, ob, r), (go, wo, w1, w2), gla=True, latent_only=last),)
    return stream[0]
```

```python
import functools
from typing import NamedTuple

import jax
import jax.numpy as jnp
from jax import lax
from jax.experimental import pallas as pl
from jax.experimental.pallas import tpu as pltpu

N_MOD = 6
NORM_EPS = 1e-6
GRID_W = 64
MLA_HEADS = 8
MLA_NOPE = 128
MLA_ROPE = 64
MLA_V = 128
MLA_VX = MLA_V + 16
MLA_Q_RANK = 384
MLA_KV_RANK = 256
ROPE_BASE = 10000.0
GLA_HEADS = 4
GLA_RANK = 16
GLA_TAU = 16.0
GLA_CHUNK = 64

LANES = 128
TILE = 256
MLA_QK = 2 * LANES
GLA_DECAY_ROWS = 8
MIB = 1024 * 1024
LOG2E = 1.4426950408889634

F32 = jnp.float32
BF16 = jnp.bfloat16
NT = (((1,), (1,)), ((), ()))


def _dot(a, b):
    return jnp.dot(a, b, preferred_element_type=F32)


def _dot_nt(a, b):
    return lax.dot_general(a, b, NT, preferred_element_type=F32)


def _rms(x, g):
    ms = jnp.mean(x * x, axis=-1, keepdims=True)
    return x * lax.rsqrt(ms + NORM_EPS) * g


class _LayerParam(NamedTuple):
    stack: jax.Array
    layer: int


def _param_spec(p):
    zeros = (0,) * (p.stack.ndim - 1)
    return pl.BlockSpec((None,) + p.stack.shape[1:], lambda *_: (p.layer,) + zeros,
                        pipeline_mode=pl.Buffered(1))


def _mod_spec(mod, group, n_tiles, n_batch):
    d = mod.stack.shape[-1]
    return pl.BlockSpec((None, group, N_MOD, d),
                        lambda bi, i: (mod.layer, jnp.where(i == n_tiles - 1, n_batch // group, bi), 0, 0))


def _params(sem, vmem_mib):
    return pltpu.CompilerParams(dimension_semantics=sem, vmem_limit_bytes=vmem_mib * MIB)


def _stream_specs(stream, n_tiles, d, nb=1):
    if len(stream) == 1:
        return [pl.BlockSpec((nb, TILE, d), lambda bi, i: (bi, i, 0))]
    last_lat = n_tiles - 2
    return [pl.BlockSpec((nb, TILE, d), lambda bi, i: (bi, jnp.minimum(i, last_lat), 0)),
            pl.BlockSpec((nb, TILE, d), lambda bi, i: (bi, 0, 0))]


def _stream_tile(stream_refs, u=0):
    if len(stream_refs) == 1:
        return stream_refs[0][u]
    is_ctx = pl.program_id(1) == pl.num_programs(1) - 1
    return jnp.where(is_ctx, stream_refs[1][u], stream_refs[0][u])


def _batch_group(b):
    return 2 if b % 2 == 0 else 1


def _ada_kernel(c_ref, w_ref, b_ref, o_ref):
    c = c_ref[...]
    a = c / (1.0 + jnp.exp(-c))
    a_hi = a.astype(BF16)
    a_lo = (a - a_hi.astype(F32)).astype(BF16)
    w = w_ref[0]
    w_hi = w.astype(BF16)
    w_lo = (w - w_hi.astype(F32)).astype(BF16)
    o_ref[0] = _dot(a_hi, w_hi) + _dot(a_lo, w_hi) + _dot(a_hi, w_lo) + b_ref[0]


def _ada(cvec, ada_w, ada_b):
    depth, d, n = ada_w.shape
    rows = cvec.shape[0]
    tn = 1536
    return pl.pallas_call(
        _ada_kernel,
        out_shape=jax.ShapeDtypeStruct((depth, rows, n), F32),
        grid=(depth, n // tn),
        in_specs=[
            pl.BlockSpec((rows, d), lambda l, j: (0, 0)),
            pl.BlockSpec((1, d, tn), lambda l, j: (l, 0, j)),
            pl.BlockSpec((1, 1, tn), lambda l, j: (l, 0, j)),
        ],
        out_specs=pl.BlockSpec((1, rows, tn), lambda l, j: (l, 0, j)),
        compiler_params=_params(("parallel", "parallel"), 40),
        name="ada",
    )(cvec, ada_w, ada_b.reshape(depth, 1, n))


def _mla_proj_kernel(*refs, n_stream, q_scale):
    (mod_ref, ng_ref, wd_ref, gq_ref, gkv_ref, wqt_ref, wk_ref, wvt_ref,
     rcs_ref, rcst_ref, qt_ref, k_ref, vt_ref) = refs[n_stream:]
    stream_refs = refs[:n_stream]
    group = range(qt_ref.shape[0])
    nq, nkv, nr = MLA_Q_RANK, MLA_KV_RANK, MLA_ROPE
    hw = MLA_HEADS * LANES
    hr = MLA_HEADS * nr

    def down(u):
        mod = mod_ref[u]
        h = (_rms(_stream_tile(stream_refs, u), ng_ref[0:1]) * (1.0 + mod[1:2]) + mod[0:1]).astype(BF16)
        return _dot(h, wd_ref[...])

    def latents(a):
        cq = _rms(a[:, :nq], gq_ref[...]).astype(BF16)
        ckv = _rms(a[:, nq:nq + nkv], gkv_ref[...]).astype(BF16)
        y = a[:, nq + nkv:] * rcs_ref[...]
        lane = lax.broadcasted_iota(jnp.int32, y.shape, 1)
        k_rope = jnp.where(lane < nr, y + pltpu.roll(y, nr, axis=1), 0.0).astype(BF16)
        return cq, ckv, k_rope

    def up(cq, ckv):
        qt = _dot_nt(wqt_ref[...], cq) * q_scale
        k_nope = _dot(ckv, wk_ref[...]).astype(BF16)
        vt = _dot_nt(wvt_ref[...], ckv).astype(BF16)
        return qt, k_nope, vt

    def emit(u, k_rope, qt, k_nope, vt):
        cos_t = rcst_ref[0:nr]
        sin_t = rcst_ref[nr:2 * nr]
        for hd in range(MLA_HEADS):
            s = hd * LANES
            r = hw + hd * nr
            qt_ref[u, hd, 0:LANES, :] = qt[s:s + LANES].astype(BF16)
            q_rope = qt[r:r + nr] * cos_t + qt[r + hr:r + hr + nr] * sin_t
            qt_ref[u, hd, LANES:LANES + nr, :] = q_rope.astype(BF16)
            qt_ref[u, hd, LANES + nr:, :] = jnp.zeros((MLA_QK - LANES - nr, TILE), BF16)
            k_ref[u, hd, :, 0:LANES] = k_nope[:, s:s + LANES]
            k_ref[u, hd, :, LANES:2 * LANES] = k_rope
            vt_ref[u, hd, 0:MLA_V] = vt[s:s + LANES]
            vt_ref[u, hd, MLA_V:MLA_VX] = jnp.ones((MLA_VX - MLA_V, TILE), BF16)

    downs = [down(u) for u in group]
    lats = [latents(downs[u]) for u in group]
    ups = [up(lats[u][0], lats[u][1]) for u in group]
    for u in group:
        emit(u, lats[u][2], *ups[u])


def _mla_proj(stream, mod, params, rcs, rcst):
    b, _, d = stream[0].shape
    t = sum(a.shape[1] for a in stream)
    nt = t // TILE
    hds = MLA_HEADS
    g = _batch_group(b)
    q_scale = (MLA_NOPE + MLA_ROPE) ** -0.5 * LOG2E
    return pl.pallas_call(
        functools.partial(_mla_proj_kernel, n_stream=len(stream), q_scale=q_scale),
        out_shape=(
            jax.ShapeDtypeStruct((b, hds, MLA_QK, t), BF16),
            jax.ShapeDtypeStruct((b, hds, t, MLA_QK), BF16),
            jax.ShapeDtypeStruct((b, hds, MLA_VX, t), BF16),
        ),
        grid=(b // g, nt),
        in_specs=_stream_specs(stream, nt, d, g) + [_mod_spec(mod, g, nt, b)]
        + [_param_spec(p) for p in params] + [
            pl.BlockSpec((TILE, LANES), lambda bi, i: (i, 0)),
            pl.BlockSpec((LANES, TILE), lambda bi, i: (0, i)),
        ],
        out_specs=(
            pl.BlockSpec((g, hds, MLA_QK, TILE), lambda bi, i: (bi, 0, 0, i)),
            pl.BlockSpec((g, hds, TILE, MLA_QK), lambda bi, i: (bi, 0, i, 0)),
            pl.BlockSpec((g, hds, MLA_VX, TILE), lambda bi, i: (bi, 0, 0, i)),
        ),
        compiler_params=_params(("parallel", "parallel"), 48),
        name="mla_proj",
    )(*stream, mod.stack, *[p.stack for p in params], rcs, rcst)


def _attn_kernel(*refs, tq, tk, n_chunks, n_tiles, aliased):
    if aliased:
        qt_ref, k_ref, vt_ref, _, o_ref, s_ref, p_ref = refs
    else:
        qt_ref, k_ref, vt_ref, o_ref, s_ref, p_ref = refs
    assert n_chunks >= 2 or n_tiles == 1

    def s_slot(j):
        return 2 if j == 0 else j % 2

    def q_start(i):
        return i * tq if isinstance(i, int) else pl.multiple_of(i * tq, tq)

    def scores(i, j):
        qt = qt_ref[0, 0, :, pl.ds(q_start(i), tq)]
        k = k_ref[0, 0, j * tk:(j + 1) * tk, :]
        s = _dot(k, qt)
        s_ref[s_slot(j)] = s
        return jnp.max(s, axis=0, keepdims=True)

    def softmax(j, m, s_max):
        m_new = jnp.maximum(m, s_max)
        p = jnp.exp2(s_ref[s_slot(j)] - m_new)
        p_ref[j % 2] = p.astype(BF16)
        return m_new, jnp.exp2(m - m_new)

    def pv(j, acc, alpha):
        vt = vt_ref[0, 0, :, j * tk:(j + 1) * tk]
        return alpha * acc + _dot(vt, p_ref[j % 2])

    def tile(i, s_max):
        m = jnp.full((1, tq), -jnp.inf, F32)
        acc = jnp.zeros((MLA_VX, tq), F32)
        alpha = None
        for j in range(n_chunks):
            if j > 0:
                acc = pv(j - 1, acc, alpha)
            if j + 1 < n_chunks:
                next_max = scores(i, j + 1)
            elif n_tiles > 1:
                next_max = scores(jnp.minimum(i + 1, n_tiles - 1), 0)
            else:
                next_max = s_max
            m, alpha = softmax(j, m, s_max)
            s_max = next_max
        acc = pv(n_chunks - 1, acc, alpha)
        o = acc[:MLA_V] * (1.0 / acc[MLA_V:MLA_V + 1])
        o_ref[0, pl.ds(q_start(i), tq), :] = o.T.astype(o_ref.dtype)
        return s_max

    first_max = scores(0, 0)
    if n_tiles == 1:
        tile(0, first_max)
    else:
        lax.fori_loop(0, n_tiles, tile, first_max)


def _attn_tk(n_keys):
    for tk in (768, 512, 256):
        if n_keys % tk == 0:
            return tk
    raise ValueError(f"unsupported key count {n_keys}")


def _mla_attention(qt, k, vt, n_lat, tq):
    b, hds, _, t = qt.shape
    out_shape = jax.ShapeDtypeStruct((b, t, hds * MLA_V), BF16)
    sem = ("parallel", "parallel")
    tk = _attn_tk(t)
    o = pl.pallas_call(
        functools.partial(_attn_kernel, tq=tq, tk=tk, n_chunks=t // tk, n_tiles=n_lat // tq, aliased=False),
        out_shape=out_shape,
        grid=(b, hds),
        in_specs=[
            pl.BlockSpec((1, 1, MLA_QK, n_lat), lambda bi, h: (bi, h, 0, 0)),
            pl.BlockSpec((1, 1, t, MLA_QK), lambda bi, h: (bi, h, 0, 0)),
            pl.BlockSpec((1, 1, MLA_VX, t), lambda bi, h: (bi, h, 0, 0)),
        ],
        out_specs=pl.BlockSpec((1, n_lat, MLA_V), lambda bi, h: (bi, 0, h)),
        scratch_shapes=[pltpu.VMEM((3, tk, tq), F32), pltpu.VMEM((2, tk, tq), BF16)],
        compiler_params=_params(sem, 56),
        name="mla_attn_lat",
    )(qt, k, vt)
    ct = n_lat // TILE
    return pl.pallas_call(
        functools.partial(_attn_kernel, tq=TILE, tk=TILE, n_chunks=1, n_tiles=1, aliased=True),
        out_shape=out_shape,
        grid=(b, hds),
        in_specs=[
            pl.BlockSpec((1, 1, MLA_QK, TILE), lambda bi, h: (bi, h, 0, ct)),
            pl.BlockSpec((1, 1, TILE, MLA_QK), lambda bi, h: (bi, h, ct, 0)),
            pl.BlockSpec((1, 1, MLA_VX, TILE), lambda bi, h: (bi, h, 0, ct)),
            pl.BlockSpec(memory_space=pl.ANY),
        ],
        out_specs=pl.BlockSpec((1, TILE, MLA_V), lambda bi, h: (bi, ct, h)),
        input_output_aliases={3: 0},
        scratch_shapes=[pltpu.VMEM((3, TILE, TILE), F32), pltpu.VMEM((2, TILE, TILE), BF16)],
        compiler_params=_params(sem, 32),
        name="mla_attn_ctx",
    )(qt, k, vt, o)


def _gla_proj_kernel(x_ref, mod_ref, ng_ref, wg_ref, w2_ref, bg_ref,
                     qf_ref, kf_ref, kfe_ref, qb_ref, kb_ref, kbe_ref, v_ref, r_ref, d_ref):
    group = range(x_ref.shape[0])
    dk = GLA_HEADS * LANES
    dv = x_ref.shape[2]
    n_chunks = TILE // GLA_CHUNK

    def project(u):
        mod = mod_ref[u]
        h = (_rms(x_ref[u], ng_ref[0:1]) * (1.0 + mod[1:2]) + mod[0:1]).astype(BF16)
        return _dot(h, wg_ref[...])

    def gates(u, big):
        v_ref[u] = big[:, 2 * dk:2 * dk + dv].astype(BF16)
        rr = big[:, 2 * dk + dv:2 * dk + 2 * dv]
        r_ref[u] = (rr / (1.0 + jnp.exp(-rr))).astype(BF16)
        low = big[:, 2 * dk + 2 * dv:].astype(BF16)
        z = _dot(low, w2_ref[...]) + bg_ref[...]
        g = -(jnp.maximum(-z, 0.0) + jnp.log(1.0 + jnp.exp(-jnp.abs(z)))) * (1.0 / GLA_TAU)
        g_hi = g.astype(BF16)
        return g_hi, (g - g_hi.astype(F32)).astype(BF16)

    row = lax.broadcasted_iota(jnp.int32, (TILE, TILE), 0)
    col = lax.broadcasted_iota(jnp.int32, (TILE, TILE), 1)
    shift = GLA_CHUNK.bit_length() - 1
    same = (row >> shift) == (col >> shift)
    prefix = jnp.where(same & (col <= row), 1.0, 0.0).astype(BF16)
    suffix = jnp.where(same & (col >= row), 1.0, 0.0).astype(BF16)

    def cumsums(g_hi, g_lo):
        gf = _dot(prefix, g_hi[:, :dk]) + _dot(prefix, g_lo[:, :dk])
        gb = _dot(suffix, g_hi[:, dk:]) + _dot(suffix, g_lo[:, dk:])
        return gf, gb

    def emit(u, big, gf, gb):
        q = big[:, :dk] * (LANES ** -0.5)
        k = big[:, dk:2 * dk]
        tot_f = [gf[(c + 1) * GLA_CHUNK - 1:(c + 1) * GLA_CHUNK] for c in range(n_chunks)]
        tot_b = [gb[c * GLA_CHUNK:c * GLA_CHUNK + 1] for c in range(n_chunks)]
        tf = jnp.concatenate([jnp.broadcast_to(t, (GLA_CHUNK, dk)) for t in tot_f], axis=0)
        tb = jnp.concatenate([jnp.broadcast_to(t, (GLA_CHUNK, dk)) for t in tot_b], axis=0)
        for c in range(n_chunks):
            d_ref[u, 0, c:c + 1, 0:dk] = jnp.exp(tot_f[c])
            d_ref[u, 0, c:c + 1, dk:2 * dk] = jnp.exp(tot_b[c])
        d_ref[u, 0, n_chunks:, :] = jnp.ones((d_ref.shape[2] - n_chunks, 2 * dk), F32)
        qf_ref[u] = (q * jnp.exp(gf)).astype(BF16)
        kf_ref[u] = (k * jnp.exp(-gf)).astype(BF16)
        kfe_ref[u] = (k * jnp.exp(tf - gf)).astype(BF16)
        qb_ref[u] = (q * jnp.exp(gb)).astype(BF16)
        kb_ref[u] = (k * jnp.exp(-gb)).astype(BF16)
        kbe_ref[u] = (k * jnp.exp(tb - gb)).astype(BF16)

    bigs = [project(u) for u in group]
    splits = [gates(u, bigs[u]) for u in group]
    sums = [cumsums(*splits[u]) for u in group]
    for u in group:
        emit(u, bigs[u], *sums[u])


def _gla_proj(xa, mod, params):
    b, t, d = xa.shape
    nt = t // TILE
    dk = GLA_HEADS * LANES
    g = _batch_group(b)
    row_spec = lambda w: pl.BlockSpec((g, TILE, w), lambda bi, i: (bi, i, 0))
    qk = jax.ShapeDtypeStruct((b, t, dk), BF16)
    wide = jax.ShapeDtypeStruct((b, t, d), BF16)
    return pl.pallas_call(
        _gla_proj_kernel,
        out_shape=(qk, qk, qk, qk, qk, qk, wide, wide,
                   jax.ShapeDtypeStruct((b, nt, GLA_DECAY_ROWS, 2 * dk), F32)),
        grid=(b // g, nt),
        in_specs=[row_spec(d), _mod_spec(mod, g, nt, b)] + [_param_spec(p) for p in params],
        out_specs=(row_spec(dk),) * 6 + (
            row_spec(d),
            row_spec(d),
            pl.BlockSpec((g, 1, GLA_DECAY_ROWS, 2 * dk), lambda bi, i: (bi, i, 0, 0)),
        ),
        compiler_params=_params(("parallel", "parallel"), 56),
        name="gla_proj",
    )(xa, mod.stack, *[p.stack for p in params])


def _gla_scan_kernel(qf_ref, kf_ref, kfe_ref, vf_ref, df_ref,
                     qb_ref, kb_ref, kbe_ref, vb_ref, db_ref,
                     of_ref, ob_ref, s_ref):
    @pl.when(pl.program_id(1) == 0)
    def _():
        s_ref[...] = jnp.zeros(s_ref.shape, s_ref.dtype)

    L = GLA_CHUNK
    n_chunks = TILE // L
    dk = LANES
    dv = vf_ref.shape[2] // GLA_HEADS
    shift = L.bit_length() - 1
    ri = lax.broadcasted_iota(jnp.int32, (TILE, TILE), 0)
    ci = lax.broadcasted_iota(jnp.int32, (TILE, TILE), 1)
    same_chunk = (ri >> shift) == (ci >> shift)
    causal = same_chunk & (ci <= ri)
    anti = same_chunk & (ci >= ri)
    token_chunk = lax.broadcasted_iota(jnp.int32, (dk, TILE), 1) >> shift

    def run(direction, q_ref, k_ref, ke_ref, v_ref, d_ref, o_ref, mask, order):
        for hd in range(GLA_HEADS):
            kc = slice(hd * dk, (hd + 1) * dk)
            vc = slice(hd * dv, (hd + 1) * dv)
            idx = direction * GLA_HEADS + hd
            q = q_ref[0, :, kc]
            v = v_ref[0, :, vc]
            a = jnp.where(mask, _dot_nt(q, k_ref[0, :, kc]), 0.0).astype(BF16)
            kt = ke_ref[0, :, kc].astype(F32).T
            kt_blocks = [jnp.where(token_chunk == c, kt, 0.0).astype(BF16) for c in range(n_chunks)]
            r = _dot(jnp.concatenate([a] + kt_blocks, axis=0), v)
            d_rows = d_ref[0, 0, :, idx * dk:(idx + 1) * dk]
            d_cols = jnp.concatenate([d_rows, jnp.zeros((dk - d_rows.shape[0], dk), F32)], axis=0).T
            st = s_ref[idx]
            for c in order:
                rows = slice(c * L, (c + 1) * L)
                o = r[rows] + _dot(q[rows], st.astype(BF16))
                o_ref[0, rows, vc] = o.astype(o_ref.dtype)
                st = st * d_cols[:, c:c + 1] + r[TILE + c * dk:TILE + (c + 1) * dk]
            s_ref[idx] = st

    run(0, qf_ref, kf_ref, kfe_ref, vf_ref, df_ref, of_ref, causal, range(n_chunks))
    run(1, qb_ref, kb_ref, kbe_ref, vb_ref, db_ref, ob_ref, anti, range(n_chunks - 1, -1, -1))


def _gla_scan(qf, kf, kfe, qb, kb, kbe, v, dd):
    b, t, dkh = qf.shape
    d = v.shape[2]
    nt = t // TILE
    f_tile = lambda s: jnp.where(s == 0, nt - 1, s - 1)
    b_tile = lambda s: jnp.where(s == 0, nt - 1, nt - 1 - s)

    def specs(tile):
        row = lambda w: pl.BlockSpec((1, TILE, w), lambda bi, s: (bi, tile(s), 0))
        return [row(dkh), row(dkh), row(dkh), row(d),
                pl.BlockSpec((1, 1, GLA_DECAY_ROWS, 2 * dkh), lambda bi, s: (bi, tile(s), 0, 0))]

    out = jax.ShapeDtypeStruct((b, t, d), BF16)
    return pl.pallas_call(
        _gla_scan_kernel,
        out_shape=(out, out),
        grid=(b, nt),
        in_specs=specs(f_tile) + specs(b_tile),
        out_specs=(pl.BlockSpec((1, TILE, d), lambda bi, s: (bi, f_tile(s), 0)),
                   pl.BlockSpec((1, TILE, d), lambda bi, s: (bi, b_tile(s), 0))),
        scratch_shapes=[pltpu.VMEM((2 * GLA_HEADS, LANES, d // GLA_HEADS), F32)],
        compiler_params=_params(("parallel", "arbitrary"), 32),
        name="gla_scan",
    )(qf, kf, kfe, v, dd, qb, kb, kbe, v, dd)


def _post_kernel(*refs, n_stream, gla):
    stream_refs = refs[:n_stream]
    refs = refs[n_stream:]
    if gla:
        mod_ref, ng_ref, of_ref, ob_ref, r_ref, go_ref, wo_ref, w1_ref, w2_ref, out_ref = refs
    else:
        mod_ref, ng_ref, o_ref, wo_ref, w1_ref, w2_ref, out_ref = refs
    ng = ng_ref[...]
    group = range(out_ref.shape[0])
    d = out_ref.shape[2]

    def mixer_proj(u):
        if gla:
            o = of_ref[u].astype(F32) + ob_ref[u].astype(F32)
            dv = d // GLA_HEADS
            go = go_ref[...]
            o = jnp.concatenate([_rms(o[:, i * dv:(i + 1) * dv], go) for i in range(GLA_HEADS)], axis=1)
            y_in = (o * r_ref[u].astype(F32)).astype(BF16)
        else:
            y_in = o_ref[u]
        return _dot(y_in, wo_ref[...])

    def residual(u, y):
        mod = mod_ref[u]
        x1 = _stream_tile(stream_refs, u) + mod[2:3] * _rms(y, ng[1:2])
        return x1, (_rms(x1, ng[2:3]) * (1.0 + mod[4:5]) + mod[3:4]).astype(BF16)

    def mlp(h):
        z = jnp.zeros((TILE, d), F32)
        for j in range(w1_ref.shape[1] // d):
            a = jnp.maximum(_dot(h, w1_ref[:, j * d:(j + 1) * d]), 0.0)
            z = z + _dot((a * a).astype(BF16), w2_ref[j * d:(j + 1) * d, :])
        return z

    ys = [mixer_proj(u) for u in group]
    zs, x1s = [], []
    for u in group:
        x1, h = residual(u, ys[u])
        x1s.append(x1)
        zs.append(mlp(h))
    for u in group:
        out_ref[u] = x1s[u] + mod_ref[u][5:6] * _rms(zs[u], ng[3:4])


def _post(stream, mod, ng, mixer_out, weights, gla, latent_only):
    b, _, d = stream[0].shape
    t = mixer_out[0].shape[1]
    nt = t // TILE
    if latent_only:
        stream = stream[:1]
    n_out = nt - 1 if latent_only else nt
    g = _batch_group(b)
    row = pl.BlockSpec((g, TILE, d), lambda bi, i: (bi, i, 0))
    in_place = len(stream) == 1 and not latent_only
    return pl.pallas_call(
        functools.partial(_post_kernel, n_stream=len(stream), gla=gla),
        out_shape=jax.ShapeDtypeStruct((b, n_out * TILE, d), stream[0].dtype),
        grid=(b // g, n_out),
        in_specs=_stream_specs(stream, nt, d, g)
        + [_mod_spec(mod, g, nt, b), _param_spec(ng)]
        + [row] * len(mixer_out) + [_param_spec(w) for w in weights],
        out_specs=row,
        input_output_aliases={0: 0} if in_place else {},
        compiler_params=_params(("parallel", "parallel"), 56),
        name="gla_post" if gla else "mla_post",
    )(*stream, mod.stack, ng.stack, *mixer_out, *[w.stack for w in weights])


def _rope_tables(n_lat, n_ctx):
    t = jnp.arange(n_lat)
    row = (t // GRID_W).astype(F32)
    col = (t % GRID_W).astype(F32)
    half = MLA_ROPE // 2
    inv_freq = ROPE_BASE ** (-jnp.arange(0, half, 2, dtype=F32) / half)
    ang_r = row[:, None] * inv_freq
    ang_c = col[:, None] * inv_freq
    cr, sr, cc, sc = jnp.cos(ang_r), jnp.sin(ang_r), jnp.cos(ang_c), jnp.sin(ang_c)
    cos = jnp.concatenate([cr, cr, cc, cc], axis=1)
    sin = jnp.concatenate([-sr, sr, -sc, sc], axis=1)
    cos = jnp.concatenate([cos, jnp.ones((n_ctx, MLA_ROPE), F32)], axis=0)
    sin = jnp.concatenate([sin, jnp.zeros((n_ctx, MLA_ROPE), F32)], axis=0)
    assert 2 * MLA_ROPE == LANES
    return jnp.concatenate([cos, sin], axis=1)


def _swap_perm():
    q = MLA_ROPE // 4
    return jnp.array(list(range(q, 2 * q)) + list(range(0, q)) + list(range(3 * q, 4 * q)) + list(range(2 * q, 3 * q)))


def _mla_weights(w_dq, g_q, w_uq, w_dkv, g_kv, w_ukv, w_o):
    n = w_dq.shape[0]
    perm = _swap_perm()
    rope = w_dkv[:, :, MLA_KV_RANK:]
    wd = jnp.concatenate([w_dq, w_dkv[:, :, :MLA_KV_RANK], rope, rope[:, :, perm]], axis=2)
    uq = w_uq.reshape(n, MLA_Q_RANK, MLA_HEADS, MLA_NOPE + MLA_ROPE)
    q_nope = uq[..., :MLA_NOPE].reshape(n, MLA_Q_RANK, -1)
    q_rope = uq[..., MLA_NOPE:]
    q_swap = q_rope[..., perm]
    wq = jnp.concatenate([q_nope, q_rope.reshape(n, MLA_Q_RANK, -1), q_swap.reshape(n, MLA_Q_RANK, -1)], axis=2)
    ukv = w_ukv.reshape(n, MLA_KV_RANK, MLA_HEADS, MLA_NOPE + MLA_V)
    wk = ukv[..., :MLA_NOPE].reshape(n, MLA_KV_RANK, -1)
    wv = ukv[..., MLA_NOPE:].reshape(n, MLA_KV_RANK, -1)
    proj = (wd.astype(BF16), g_q[:, None, :], g_kv[:, None, :], wq.astype(BF16).transpose(0, 2, 1),
            wk.astype(BF16), wv.astype(BF16).transpose(0, 2, 1))
    return proj, w_o.astype(BF16)


def _gla_weights(w_q, w_k, w_v, w_r, w_gate1, w_gate2, b_gate, g_o, w_o):
    n, d, dk = w_q.shape
    low_pad = jnp.zeros((n, d, LANES - 2 * GLA_RANK), w_q.dtype)
    wg = jnp.concatenate([w_q, w_k, w_v, w_r, w_gate1[:, 0], w_gate1[:, 1], low_pad], axis=2)
    zero = jnp.zeros((n, GLA_RANK, dk), w_gate2.dtype)
    w2 = jnp.concatenate([
        jnp.concatenate([w_gate2[:, 0], zero], axis=2),
        jnp.concatenate([zero, w_gate2[:, 1]], axis=2),
        jnp.zeros((n, LANES - 2 * GLA_RANK, 2 * dk), w_gate2.dtype)], axis=1)
    bg = jnp.concatenate([b_gate[:, 0], b_gate[:, 1]], axis=1)[:, None, :]
    return (wg.astype(BF16), w2.astype(BF16), bg), g_o[:, None, :], w_o.astype(BF16)


def kernel(x, c, ctx, c_ctx, ada_w, ada_b, norm_g, mlp_w1, mlp_w2, mla_w_dq, mla_g_q, mla_w_uq, mla_w_dkv, mla_g_kv, mla_w_ukv, mla_w_o, gla_w_q, gla_w_k, gla_w_v, gla_w_r, gla_w_gate1, gla_w_gate2, gla_b_gate, gla_g_o, gla_w_o):
    b, s, d = x.shape
    n_ctx = ctx.shape[1]
    depth = ada_w.shape[0]
    assert n_ctx == TILE and s % TILE == 0 and d == GLA_HEADS * 2 * LANES

    stream = (x, ctx)
    g = _batch_group(b)
    rows = -(-(b + g) // 16) * 16
    cvec = jnp.concatenate([c, jnp.tile(c_ctx[None], (g, 1)), jnp.zeros((rows - b - g, d), c.dtype)], axis=0)
    mod_all = _ada(cvec, ada_w, ada_b).reshape(depth, rows, N_MOD, d)

    rcs = _rope_tables(s, n_ctx)
    rcst = rcs.T
    tq = 512 if s % 512 == 0 else TILE

    w1_all = mlp_w1.astype(BF16)
    w2_all = mlp_w2.astype(BF16)
    mla_proj_w, mla_wo = _mla_weights(mla_w_dq, mla_g_q, mla_w_uq, mla_w_dkv, mla_g_kv, mla_w_ukv, mla_w_o)
    gla_proj_w, gla_go, gla_wo = _gla_weights(gla_w_q, gla_w_k, gla_w_v, gla_w_r, gla_w_gate1, gla_w_gate2,
                                              gla_b_gate, gla_g_o, gla_w_o)

    for i in range(depth):
        j = i // 2
        mod = _LayerParam(mod_all, i)
        ng = _LayerParam(norm_g, i)
        mlp = (_LayerParam(w1_all, i), _LayerParam(w2_all, i))
        last = i == depth - 1
        if i % 2 == 0:
            qt, k, vt = _mla_proj(stream, mod, [ng] + [_LayerParam(w, j) for w in mla_proj_w], rcs, rcst)
            o = _mla_attention(qt, k, vt, s, tq)
            weights = (_LayerParam(mla_wo, j),) + mlp
            stream = (_post(stream, mod, ng, (o,), weights, gla=False, latent_only=last),)
        else:
            (xa,) = stream
            qf, kf, kfe, qb, kb, kbe, v, r, dd = _gla_proj(xa, mod, [ng] + [_LayerParam(w, j) for w in gla_proj_w])
            of, ob = _gla_scan(qf, kf, kfe, qb, kb, kbe, v, dd)
            weights = (_LayerParam(gla_go, j), _LayerParam(gla_wo, j)) + mlp
            stream = (_post(stream, mod, ng, (of, ob, r), weights, gla=True, latent_only=last),)
    return stream[0]
```

```python
import functools
from typing import NamedTuple

import jax
import jax.numpy as jnp
from jax import lax
from jax.experimental import pallas as pl
from jax.experimental.pallas import tpu as pltpu

N_MOD = 6
NORM_EPS = 1e-6
GRID_W = 64
MLA_HEADS = 8
MLA_NOPE = 128
MLA_ROPE = 64
MLA_V = 128
MLA_VX = MLA_V + 16
MLA_Q_RANK = 384
MLA_KV_RANK = 256
ROPE_BASE = 10000.0
GLA_HEADS = 4
GLA_RANK = 16
GLA_TAU = 16.0
GLA_CHUNK = 64

LANES = 128
TILE = 256
MLA_QK = 2 * LANES
GLA_DECAY_ROWS = 8
MIB = 1024 * 1024
LOG2E = 1.4426950408889634

F32 = jnp.float32
BF16 = jnp.bfloat16
NT = (((1,), (1,)), ((), ()))


def _dot(a, b):
    return jnp.dot(a, b, preferred_element_type=F32)


def _dot_nt(a, b):
    return lax.dot_general(a, b, NT, preferred_element_type=F32)


def _rms(x, g):
    ms = jnp.mean(x * x, axis=-1, keepdims=True)
    return x * lax.rsqrt(ms + NORM_EPS) * g


class _LayerParam(NamedTuple):
    stack: jax.Array
    layer: int


def _param_spec(p):
    zeros = (0,) * (p.stack.ndim - 1)
    return pl.BlockSpec((None,) + p.stack.shape[1:], lambda *_: (p.layer,) + zeros,
                        pipeline_mode=pl.Buffered(1))


def _mod_spec(mod, group, n_tiles, n_batch):
    d = mod.stack.shape[-1]
    return pl.BlockSpec((None, group, N_MOD, d),
                        lambda bi, i: (mod.layer, jnp.where(i == n_tiles - 1, n_batch // group, bi), 0, 0))


def _params(sem, vmem_mib):
    return pltpu.CompilerParams(dimension_semantics=sem, vmem_limit_bytes=vmem_mib * MIB)


def _stream_specs(stream, n_tiles, d, nb=1):
    if len(stream) == 1:
        return [pl.BlockSpec((nb, TILE, d), lambda bi, i: (bi, i, 0))]
    last_lat = n_tiles - 2
    return [pl.BlockSpec((nb, TILE, d), lambda bi, i: (bi, jnp.minimum(i, last_lat), 0)),
            pl.BlockSpec((nb, TILE, d), lambda bi, i: (bi, 0, 0))]


def _stream_tile(stream_refs, u=0):
    if len(stream_refs) == 1:
        return stream_refs[0][u]
    is_ctx = pl.program_id(1) == pl.num_programs(1) - 1
    return jnp.where(is_ctx, stream_refs[1][u], stream_refs[0][u])


def _batch_group(b):
    return 2 if b % 2 == 0 else 1


def _ada_kernel(c_ref, w_ref, b_ref, o_ref):
    c = c_ref[...]
    a = c / (1.0 + jnp.exp(-c))
    a_hi = a.astype(BF16)
    a_lo = (a - a_hi.astype(F32)).astype(BF16)
    w = w_ref[0]
    w_hi = w.astype(BF16)
    w_lo = (w - w_hi.astype(F32)).astype(BF16)
    o_ref[0] = _dot(a_hi, w_hi) + _dot(a_lo, w_hi) + _dot(a_hi, w_lo) + b_ref[0]


def _ada(cvec, ada_w, ada_b):
    depth, d, n = ada_w.shape
    rows = cvec.shape[0]
    tn = 1536
    return pl.pallas_call(
        _ada_kernel,
        out_shape=jax.ShapeDtypeStruct((depth, rows, n), F32),
        grid=(depth, n // tn),
        in_specs=[
            pl.BlockSpec((rows, d), lambda l, j: (0, 0)),
            pl.BlockSpec((1, d, tn), lambda l, j: (l, 0, j)),
            pl.BlockSpec((1, 1, tn), lambda l, j: (l, 0, j)),
        ],
        out_specs=pl.BlockSpec((1, rows, tn), lambda l, j: (l, 0, j)),
        compiler_params=_params(("parallel", "parallel"), 40),
        name="ada",
    )(cvec, ada_w, ada_b.reshape(depth, 1, n))


def _mla_proj_kernel(*refs, n_stream, q_scale):
    (mod_ref, ng_ref, wd_ref, gq_ref, gkv_ref, wqt_ref, wk_ref, wvt_ref,
     rcs_ref, rcst_ref, qt_ref, k_ref, vt_ref) = refs[n_stream:]
    stream_refs = refs[:n_stream]
    group = range(qt_ref.shape[0])
    nq, nkv, nr = MLA_Q_RANK, MLA_KV_RANK, MLA_ROPE
    hw = MLA_HEADS * LANES
    hr = MLA_HEADS * nr

    def down(u):
        mod = mod_ref[u]
        h = (_rms(_stream_tile(stream_refs, u), ng_ref[0:1]) * (1.0 + mod[1:2]) + mod[0:1]).astype(BF16)
        return _dot(h, wd_ref[...])

    def latents(a):
        cq = _rms(a[:, :nq], gq_ref[...]).astype(BF16)
        ckv = _rms(a[:, nq:nq + nkv], gkv_ref[...]).astype(BF16)
        y = a[:, nq + nkv:] * rcs_ref[...]
        lane = lax.broadcasted_iota(jnp.int32, y.shape, 1)
        k_rope = jnp.where(lane < nr, y + pltpu.roll(y, nr, axis=1), 0.0).astype(BF16)
        return cq, ckv, k_rope

    def up(cq, ckv):
        qt = _dot_nt(wqt_ref[...], cq) * q_scale
        k_nope = _dot(ckv, wk_ref[...]).astype(BF16)
        vt = _dot_nt(wvt_ref[...], ckv).astype(BF16)
        return qt, k_nope, vt

    def emit(u, k_rope, qt, k_nope, vt):
        cos_t = rcst_ref[0:nr]
        sin_t = rcst_ref[nr:2 * nr]
        for hd in range(MLA_HEADS):
            s = hd * LANES
            r = hw + hd * nr
            qt_ref[u, hd, 0:LANES, :] = qt[s:s + LANES].astype(BF16)
            q_rope = qt[r:r + nr] * cos_t + qt[r + hr:r + hr + nr] * sin_t
            qt_ref[u, hd, LANES:LANES + nr, :] = q_rope.astype(BF16)
            qt_ref[u, hd, LANES + nr:, :] = jnp.zeros((MLA_QK - LANES - nr, TILE), BF16)
            k_ref[u, hd, :, 0:LANES] = k_nope[:, s:s + LANES]
            k_ref[u, hd, :, LANES:2 * LANES] = k_rope
            vt_ref[u, hd, 0:MLA_V] = vt[s:s + LANES]
            vt_ref[u, hd, MLA_V:MLA_VX] = jnp.ones((MLA_VX - MLA_V, TILE), BF16)

    downs = [down(u) for u in group]
    lats = [latents(downs[u]) for u in group]
    ups = [up(lats[u][0], lats[u][1]) for u in group]
    for u in group:
        emit(u, lats[u][2], *ups[u])


def _mla_proj(stream, mod, params, rcs, rcst):
    b, _, d = stream[0].shape
    t = sum(a.shape[1] for a in stream)
    nt = t // TILE
    hds = MLA_HEADS
    g = _batch_group(b)
    q_scale = (MLA_NOPE + MLA_ROPE) ** -0.5 * LOG2E
    return pl.pallas_call(
        functools.partial(_mla_proj_kernel, n_stream=len(stream), q_scale=q_scale),
        out_shape=(
            jax.ShapeDtypeStruct((b, hds, MLA_QK, t), BF16),
            jax.ShapeDtypeStruct((b, hds, t, MLA_QK), BF16),
            jax.ShapeDtypeStruct((b, hds, MLA_VX, t), BF16),
        ),
        grid=(b // g, nt),
        in_specs=_stream_specs(stream, nt, d, g) + [_mod_spec(mod, g, nt, b)]
        + [_param_spec(p) for p in params] + [
            pl.BlockSpec((TILE, LANES), lambda bi, i: (i, 0)),
            pl.BlockSpec((LANES, TILE), lambda bi, i: (0, i)),
        ],
        out_specs=(
            pl.BlockSpec((g, hds, MLA_QK, TILE), lambda bi, i: (bi, 0, 0, i)),
            pl.BlockSpec((g, hds, TILE, MLA_QK), lambda bi, i: (bi, 0, i, 0)),
            pl.BlockSpec((g, hds, MLA_VX, TILE), lambda bi, i: (bi, 0, 0, i)),
        ),
        compiler_params=_params(("parallel", "parallel"), 48),
        name="mla_proj",
    )(*stream, mod.stack, *[p.stack for p in params], rcs, rcst)


def _attn_kernel(qt_ref, k_ref, vt_ref, o_ref, s_ref, p_ref, *, tq, tk, n_tiles, n_ctx):
    n_keys = k_ref.shape[2]
    n_chunks = n_keys // tk
    assert n_chunks >= 2

    def s_slot(j):
        return 2 if j == 0 else j % 2

    def q_start(i):
        return i * tq if isinstance(i, int) else pl.multiple_of(i * tq, tq)

    def scores(i, j):
        qt = qt_ref[0, 0, :, pl.ds(q_start(i), tq)]
        k = k_ref[0, 0, j * tk:(j + 1) * tk, :]
        s = _dot(k, qt)
        s_ref[s_slot(j)] = s
        return jnp.max(s, axis=0, keepdims=True)

    def softmax(j, m, s_max):
        m_new = jnp.maximum(m, s_max)
        p = jnp.exp2(s_ref[s_slot(j)] - m_new)
        p_ref[j % 2] = p.astype(BF16)
        return m_new, jnp.exp2(m - m_new)

    def pv(j, acc, alpha):
        vt = vt_ref[0, 0, :, j * tk:(j + 1) * tk]
        return alpha * acc + _dot(vt, p_ref[j % 2])

    def tile(i, s_max):
        m = jnp.full((1, tq), -jnp.inf, F32)
        acc = jnp.zeros((MLA_VX, tq), F32)
        alpha = None
        for j in range(n_chunks):
            if j > 0:
                acc = pv(j - 1, acc, alpha)
            if j + 1 < n_chunks:
                next_max = scores(i, j + 1)
            else:
                next_max = scores(jnp.minimum(i + 1, n_tiles - 1), 0)
            m, alpha = softmax(j, m, s_max)
            s_max = next_max
        acc = pv(n_chunks - 1, acc, alpha)
        o = acc[:MLA_V] * (1.0 / acc[MLA_V:MLA_V + 1])
        o_ref[0, pl.ds(q_start(i), tq), :] = o.T.astype(o_ref.dtype)
        return s_max

    lax.fori_loop(0, n_tiles, tile, scores(0, 0))

    c0 = n_keys - n_ctx
    s = _dot(k_ref[0, 0, c0:, :], qt_ref[0, 0, :, c0:])
    p = jnp.exp2(s - jnp.max(s, axis=0, keepdims=True)).astype(BF16)
    acc = _dot(vt_ref[0, 0, :, c0:], p)
    o = acc[:MLA_V] * (1.0 / acc[MLA_V:MLA_V + 1])
    o_ref[0, c0:, :] = o.T.astype(o_ref.dtype)


def _attn_tk(n_keys):
    for tk in (768, 512, 256):
        if n_keys % tk == 0:
            return tk
    raise ValueError(f"unsupported key count {n_keys}")


def _mla_attention(qt, k, vt, n_lat, tq):
    b, hds, _, t = qt.shape
    tk = _attn_tk(t)
    return pl.pallas_call(
        functools.partial(_attn_kernel, tq=tq, tk=tk, n_tiles=n_lat // tq, n_ctx=t - n_lat),
        out_shape=jax.ShapeDtypeStruct((b, t, hds * MLA_V), BF16),
        grid=(b, hds),
        in_specs=[
            pl.BlockSpec((1, 1, MLA_QK, t), lambda bi, h: (bi, h, 0, 0)),
            pl.BlockSpec((1, 1, t, MLA_QK), lambda bi, h: (bi, h, 0, 0)),
            pl.BlockSpec((1, 1, MLA_VX, t), lambda bi, h: (bi, h, 0, 0)),
        ],
        out_specs=pl.BlockSpec((1, t, MLA_V), lambda bi, h: (bi, 0, h)),
        scratch_shapes=[pltpu.VMEM((3, tk, tq), F32), pltpu.VMEM((2, tk, tq), BF16)],
        compiler_params=_params(("parallel", "parallel"), 56),
        name="mla_attn",
    )(qt, k, vt)


def _gla_proj_kernel(x_ref, mod_ref, ng_ref, wg_ref, w2_ref, bg_ref,
                     qf_ref, kf_ref, kfe_ref, qb_ref, kb_ref, kbe_ref, v_ref, r_ref, d_ref):
    group = range(x_ref.shape[0])
    dk = GLA_HEADS * LANES
    dv = x_ref.shape[2]
    n_chunks = TILE // GLA_CHUNK

    def project(u):
        mod = mod_ref[u]
        h = (_rms(x_ref[u], ng_ref[0:1]) * (1.0 + mod[1:2]) + mod[0:1]).astype(BF16)
        return _dot(h, wg_ref[...])

    def gates(u, big):
        v_ref[u] = big[:, 2 * dk:2 * dk + dv].astype(BF16)
        rr = big[:, 2 * dk + dv:2 * dk + 2 * dv]
        r_ref[u] = (rr / (1.0 + jnp.exp(-rr))).astype(BF16)
        low = big[:, 2 * dk + 2 * dv:].astype(BF16)
        z = _dot(low, w2_ref[...]) + bg_ref[...]
        g = -(jnp.maximum(-z, 0.0) + jnp.log(1.0 + jnp.exp(-jnp.abs(z)))) * (1.0 / GLA_TAU)
        g_hi = g.astype(BF16)
        return g_hi, (g - g_hi.astype(F32)).astype(BF16)

    row = lax.broadcasted_iota(jnp.int32, (TILE, TILE), 0)
    col = lax.broadcasted_iota(jnp.int32, (TILE, TILE), 1)
    shift = GLA_CHUNK.bit_length() - 1
    same = (row >> shift) == (col >> shift)
    prefix = jnp.where(same & (col <= row), 1.0, 0.0).astype(BF16)
    suffix = jnp.where(same & (col >= row), 1.0, 0.0).astype(BF16)

    def cumsums(g_hi, g_lo):
        gf = _dot(prefix, g_hi[:, :dk]) + _dot(prefix, g_lo[:, :dk])
        gb = _dot(suffix, g_hi[:, dk:]) + _dot(suffix, g_lo[:, dk:])
        return gf, gb

    def emit(u, big, gf, gb):
        q = big[:, :dk] * (LANES ** -0.5)
        k = big[:, dk:2 * dk]
        tot_f = [gf[(c + 1) * GLA_CHUNK - 1:(c + 1) * GLA_CHUNK] for c in range(n_chunks)]
        tot_b = [gb[c * GLA_CHUNK:c * GLA_CHUNK + 1] for c in range(n_chunks)]
        dec_f = [jnp.exp(t) for t in tot_f]
        dec_b = [jnp.exp(t) for t in tot_b]
        for c in range(n_chunks):
            d_ref[u, 0, c:c + 1, 0:dk] = dec_f[c]
            d_ref[u, 0, c:c + 1, dk:2 * dk] = dec_b[c]
        d_ref[u, 0, n_chunks:, :] = jnp.ones((d_ref.shape[2] - n_chunks, 2 * dk), F32)
        kf = k * jnp.exp(-gf)
        kb = k * jnp.exp(-gb)
        qf_ref[u] = (q * jnp.exp(gf)).astype(BF16)
        kf_ref[u] = kf.astype(BF16)
        kfe_ref[u] = (kf * jnp.concatenate([jnp.broadcast_to(t, (GLA_CHUNK, dk)) for t in dec_f], axis=0)).astype(BF16)
        qb_ref[u] = (q * jnp.exp(gb)).astype(BF16)
        kb_ref[u] = kb.astype(BF16)
        kbe_ref[u] = (kb * jnp.concatenate([jnp.broadcast_to(t, (GLA_CHUNK, dk)) for t in dec_b], axis=0)).astype(BF16)

    pending = None
    for u in group:
        big = project(u)
        if pending is not None:
            emit(pending[0], pending[1], *cumsums(*pending[2]))
        pending = (u, big, gates(u, big))
    emit(pending[0], pending[1], *cumsums(*pending[2]))


def _gla_proj(xa, mod, params):
    b, t, d = xa.shape
    nt = t // TILE
    dk = GLA_HEADS * LANES
    g = _batch_group(b)
    row_spec = lambda w: pl.BlockSpec((g, TILE, w), lambda bi, i: (bi, i, 0))
    qk = jax.ShapeDtypeStruct((b, t, dk), BF16)
    wide = jax.ShapeDtypeStruct((b, t, d), BF16)
    return pl.pallas_call(
        _gla_proj_kernel,
        out_shape=(qk, qk, qk, qk, qk, qk, wide, wide,
                   jax.ShapeDtypeStruct((b, nt, GLA_DECAY_ROWS, 2 * dk), F32)),
        grid=(b // g, nt),
        in_specs=[row_spec(d), _mod_spec(mod, g, nt, b)] + [_param_spec(p) for p in params],
        out_specs=(row_spec(dk),) * 6 + (
            row_spec(d),
            row_spec(d),
            pl.BlockSpec((g, 1, GLA_DECAY_ROWS, 2 * dk), lambda bi, i: (bi, i, 0, 0)),
        ),
        compiler_params=_params(("parallel", "parallel"), 56),
        name="gla_proj",
    )(xa, mod.stack, *[p.stack for p in params])


def _gla_scan_kernel(qf_ref, kf_ref, kfe_ref, vf_ref, df_ref,
                     qb_ref, kb_ref, kbe_ref, vb_ref, db_ref,
                     of_ref, ob_ref, s_ref):
    @pl.when(pl.program_id(1) == 0)
    def _():
        s_ref[...] = jnp.zeros(s_ref.shape, s_ref.dtype)

    L = GLA_CHUNK
    n_chunks = TILE // L
    dk = LANES
    dv = vf_ref.shape[2] // GLA_HEADS
    shift = L.bit_length() - 1
    ri = lax.broadcasted_iota(jnp.int32, (TILE, TILE), 0)
    ci = lax.broadcasted_iota(jnp.int32, (TILE, TILE), 1)
    same_chunk = (ri >> shift) == (ci >> shift)
    causal = same_chunk & (ci <= ri)
    anti = same_chunk & (ci >= ri)
    token_chunk = lax.broadcasted_iota(jnp.int32, (dk, TILE), 1) >> shift

    directions = (
        (qf_ref, kf_ref, kfe_ref, vf_ref, df_ref, of_ref, causal, range(n_chunks)),
        (qb_ref, kb_ref, kbe_ref, vb_ref, db_ref, ob_ref, anti, range(n_chunks - 1, -1, -1)),
    )
    units = [(u, direction, hd) for u in range(qf_ref.shape[0]) for direction in range(2)
             for hd in range(GLA_HEADS)]

    def intra(u, direction, hd):
        q_ref, k_ref, _, _, _, _, mask, _ = directions[direction]
        kc = slice(hd * dk, (hd + 1) * dk)
        return jnp.where(mask, _dot_nt(q_ref[u, :, kc], k_ref[u, :, kc]), 0.0).astype(BF16)

    def values(u, direction, hd, a):
        _, _, ke_ref, v_ref, _, _, _, _ = directions[direction]
        kt = ke_ref[u, :, hd * dk:(hd + 1) * dk].astype(F32).T
        kt_blocks = [jnp.where(token_chunk == c, kt, 0.0).astype(BF16) for c in range(n_chunks)]
        return _dot(jnp.concatenate([a] + kt_blocks, axis=0), v_ref[u, :, hd * dv:(hd + 1) * dv])

    def recur(u, direction, hd, r):
        q_ref, _, _, _, d_ref, o_ref, _, order = directions[direction]
        col = direction * GLA_HEADS + hd
        idx = u * 2 * GLA_HEADS + col
        d_rows = d_ref[u, 0, :, col * dk:(col + 1) * dk]
        d_cols = jnp.concatenate([d_rows, jnp.zeros((dk - d_rows.shape[0], dk), F32)], axis=0).T
        st = s_ref[idx]
        for c in order:
            rows = slice(c * L, (c + 1) * L)
            o = r[rows] + _dot(q_ref[u, rows, hd * dk:(hd + 1) * dk], st.astype(BF16))
            o_ref[u, rows, hd * dv:(hd + 1) * dv] = o.astype(o_ref.dtype)
            st = st * d_cols[:, c:c + 1] + r[TILE + c * dk:TILE + (c + 1) * dk]
        s_ref[idx] = st

    a_all = [intra(*u) for u in units]
    r_all = [values(*u, a) for u, a in zip(units, a_all)]
    for u, r in zip(units, r_all):
        recur(*u, r)


def _gla_scan(qf, kf, kfe, qb, kb, kbe, v, dd):
    b, t, dkh = qf.shape
    d = v.shape[2]
    nt = t // TILE
    f_tile = lambda s: jnp.where(s == 0, nt - 1, s - 1)
    b_tile = lambda s: jnp.where(s == 0, nt - 1, nt - 1 - s)

    g = _batch_group(b)

    def specs(tile):
        row = lambda w: pl.BlockSpec((g, TILE, w), lambda bi, s: (bi, tile(s), 0))
        return [row(dkh), row(dkh), row(dkh), row(d),
                pl.BlockSpec((g, 1, GLA_DECAY_ROWS, 2 * dkh), lambda bi, s: (bi, tile(s), 0, 0))]

    out = jax.ShapeDtypeStruct((b, t, d), BF16)
    return pl.pallas_call(
        _gla_scan_kernel,
        out_shape=(out, out),
        grid=(b // g, nt),
        in_specs=specs(f_tile) + specs(b_tile),
        out_specs=(pl.BlockSpec((g, TILE, d), lambda bi, s: (bi, f_tile(s), 0)),
                   pl.BlockSpec((g, TILE, d), lambda bi, s: (bi, b_tile(s), 0))),
        scratch_shapes=[pltpu.VMEM((g * 2 * GLA_HEADS, LANES, d // GLA_HEADS), F32)],
        compiler_params=_params(("parallel", "arbitrary"), 48),
        name="gla_scan",
    )(qf, kf, kfe, v, dd, qb, kb, kbe, v, dd)


def _post_kernel(*refs, n_stream, gla):
    stream_refs = refs[:n_stream]
    refs = refs[n_stream:]
    if gla:
        mod_ref, ng_ref, of_ref, ob_ref, r_ref, go_ref, wo_ref, w1_ref, w2_ref, out_ref = refs
    else:
        mod_ref, ng_ref, o_ref, wo_ref, w1_ref, w2_ref, out_ref = refs
    ng = ng_ref[...]
    group = range(out_ref.shape[0])
    d = out_ref.shape[2]

    def mixer_proj(u):
        if gla:
            o = of_ref[u].astype(F32) + ob_ref[u].astype(F32)
            dv = d // GLA_HEADS
            go = go_ref[...]
            o = jnp.concatenate([_rms(o[:, i * dv:(i + 1) * dv], go) for i in range(GLA_HEADS)], axis=1)
            y_in = (o * r_ref[u].astype(F32)).astype(BF16)
        else:
            y_in = o_ref[u]
        return _dot(y_in, wo_ref[...])

    def residual(u, y):
        mod = mod_ref[u]
        x1 = _stream_tile(stream_refs, u) + mod[2:3] * _rms(y, ng[1:2])
        return x1, (_rms(x1, ng[2:3]) * (1.0 + mod[4:5]) + mod[3:4]).astype(BF16)

    def mlp(h):
        z = jnp.zeros((TILE, d), F32)
        for j in range(w1_ref.shape[1] // d):
            a = jnp.maximum(_dot(h, w1_ref[:, j * d:(j + 1) * d]), 0.0)
            z = z + _dot((a * a).astype(BF16), w2_ref[j * d:(j + 1) * d, :])
        return z

    ys = [mixer_proj(u) for u in group]
    zs, x1s = [], []
    for u in group:
        x1, h = residual(u, ys[u])
        x1s.append(x1)
        zs.append(mlp(h))
    for u in group:
        out_ref[u] = x1s[u] + mod_ref[u][5:6] * _rms(zs[u], ng[3:4])


def _post(stream, mod, ng, mixer_out, weights, gla, latent_only):
    b, _, d = stream[0].shape
    t = mixer_out[0].shape[1]
    nt = t // TILE
    if latent_only:
        stream = stream[:1]
    n_out = nt - 1 if latent_only else nt
    g = _batch_group(b)
    row = pl.BlockSpec((g, TILE, d), lambda bi, i: (bi, i, 0))
    in_place = len(stream) == 1 and not latent_only
    return pl.pallas_call(
        functools.partial(_post_kernel, n_stream=len(stream), gla=gla),
        out_shape=jax.ShapeDtypeStruct((b, n_out * TILE, d), stream[0].dtype),
        grid=(b // g, n_out),
        in_specs=_stream_specs(stream, nt, d, g)
        + [_mod_spec(mod, g, nt, b), _param_spec(ng)]
        + [row] * len(mixer_out) + [_param_spec(w) for w in weights],
        out_specs=row,
        input_output_aliases={0: 0} if in_place else {},
        compiler_params=_params(("parallel", "parallel"), 56),
        name="gla_post" if gla else "mla_post",
    )(*stream, mod.stack, ng.stack, *mixer_out, *[w.stack for w in weights])


def _rope_tables(n_lat, n_ctx):
    t = jnp.arange(n_lat)
    row = (t // GRID_W).astype(F32)
    col = (t % GRID_W).astype(F32)
    half = MLA_ROPE // 2
    inv_freq = ROPE_BASE ** (-jnp.arange(0, half, 2, dtype=F32) / half)
    ang_r = row[:, None] * inv_freq
    ang_c = col[:, None] * inv_freq
    cr, sr, cc, sc = jnp.cos(ang_r), jnp.sin(ang_r), jnp.cos(ang_c), jnp.sin(ang_c)
    cos = jnp.concatenate([cr, cr, cc, cc], axis=1)
    sin = jnp.concatenate([-sr, sr, -sc, sc], axis=1)
    cos = jnp.concatenate([cos, jnp.ones((n_ctx, MLA_ROPE), F32)], axis=0)
    sin = jnp.concatenate([sin, jnp.zeros((n_ctx, MLA_ROPE), F32)], axis=0)
    assert 2 * MLA_ROPE == LANES
    return jnp.concatenate([cos, sin], axis=1)


def _swap_perm():
    q = MLA_ROPE // 4
    return jnp.array(list(range(q, 2 * q)) + list(range(0, q)) + list(range(3 * q, 4 * q)) + list(range(2 * q, 3 * q)))


def _mla_weights(w_dq, g_q, w_uq, w_dkv, g_kv, w_ukv, w_o):
    n = w_dq.shape[0]
    perm = _swap_perm()
    rope = w_dkv[:, :, MLA_KV_RANK:]
    wd = jnp.concatenate([w_dq, w_dkv[:, :, :MLA_KV_RANK], rope, rope[:, :, perm]], axis=2)
    uq = w_uq.reshape(n, MLA_Q_RANK, MLA_HEADS, MLA_NOPE + MLA_ROPE)
    q_nope = uq[..., :MLA_NOPE].reshape(n, MLA_Q_RANK, -1)
    q_rope = uq[..., MLA_NOPE:]
    q_swap = q_rope[..., perm]
    wq = jnp.concatenate([q_nope, q_rope.reshape(n, MLA_Q_RANK, -1), q_swap.reshape(n, MLA_Q_RANK, -1)], axis=2)
    ukv = w_ukv.reshape(n, MLA_KV_RANK, MLA_HEADS, MLA_NOPE + MLA_V)
    wk = ukv[..., :MLA_NOPE].reshape(n, MLA_KV_RANK, -1)
    wv = ukv[..., MLA_NOPE:].reshape(n, MLA_KV_RANK, -1)
    proj = (wd.astype(BF16), g_q[:, None, :], g_kv[:, None, :], wq.astype(BF16).transpose(0, 2, 1),
            wk.astype(BF16), wv.astype(BF16).transpose(0, 2, 1))
    return proj, w_o.astype(BF16)


def _gla_weights(w_q, w_k, w_v, w_r, w_gate1, w_gate2, b_gate, g_o, w_o):
    n, d, dk = w_q.shape
    low_pad = jnp.zeros((n, d, LANES - 2 * GLA_RANK), w_q.dtype)
    wg = jnp.concatenate([w_q, w_k, w_v, w_r, w_gate1[:, 0], w_gate1[:, 1], low_pad], axis=2)
    zero = jnp.zeros((n, GLA_RANK, dk), w_gate2.dtype)
    w2 = jnp.concatenate([
        jnp.concatenate([w_gate2[:, 0], zero], axis=2),
        jnp.concatenate([zero, w_gate2[:, 1]], axis=2),
        jnp.zeros((n, LANES - 2 * GLA_RANK, 2 * dk), w_gate2.dtype)], axis=1)
    bg = jnp.concatenate([b_gate[:, 0], b_gate[:, 1]], axis=1)[:, None, :]
    return (wg.astype(BF16), w2.astype(BF16), bg), g_o[:, None, :], w_o.astype(BF16)


def kernel(x, c, ctx, c_ctx, ada_w, ada_b, norm_g, mlp_w1, mlp_w2, mla_w_dq, mla_g_q, mla_w_uq, mla_w_dkv, mla_g_kv, mla_w_ukv, mla_w_o, gla_w_q, gla_w_k, gla_w_v, gla_w_r, gla_w_gate1, gla_w_gate2, gla_b_gate, gla_g_o, gla_w_o):
    b, s, d = x.shape
    n_ctx = ctx.shape[1]
    depth = ada_w.shape[0]
    assert n_ctx == TILE and s % TILE == 0 and d == GLA_HEADS * 2 * LANES

    stream = (x, ctx)
    g = _batch_group(b)
    rows = -(-(b + g) // 16) * 16
    cvec = jnp.concatenate([c, jnp.tile(c_ctx[None], (g, 1)), jnp.zeros((rows - b - g, d), c.dtype)], axis=0)
    mod_all = _ada(cvec, ada_w, ada_b).reshape(depth, rows, N_MOD, d)

    rcs = _rope_tables(s, n_ctx)
    rcst = rcs.T
    tq = 512 if s % 512 == 0 else TILE

    w1_all = mlp_w1.astype(BF16)
    w2_all = mlp_w2.astype(BF16)
    mla_proj_w, mla_wo = _mla_weights(mla_w_dq, mla_g_q, mla_w_uq, mla_w_dkv, mla_g_kv, mla_w_ukv, mla_w_o)
    gla_proj_w, gla_go, gla_wo = _gla_weights(gla_w_q, gla_w_k, gla_w_v, gla_w_r, gla_w_gate1, gla_w_gate2,
                                              gla_b_gate, gla_g_o, gla_w_o)

    for i in range(depth):
        j = i // 2
        mod = _LayerParam(mod_all, i)
        ng = _LayerParam(norm_g, i)
        mlp = (_LayerParam(w1_all, i), _LayerParam(w2_all, i))
        last = i == depth - 1
        if i % 2 == 0:
            qt, k, vt = _mla_proj(stream, mod, [ng] + [_LayerParam(w, j) for w in mla_proj_w], rcs, rcst)
            o = _mla_attention(qt, k, vt, s, tq)
            weights = (_LayerParam(mla_wo, j),) + mlp
            stream = (_post(stream, mod, ng, (o,), weights, gla=False, latent_only=last),)
        else:
            (xa,) = stream
            qf, kf, kfe, qb, kb, kbe, v, r, dd = _gla_proj(xa, mod, [ng] + [_LayerParam(w, j) for w in gla_proj_w])
            of, ob = _gla_scan(qf, kf, kfe, qb, kb, kbe, v, dd)
            weights = (_LayerParam(gla_go, j), _LayerParam(gla_wo, j)) + mlp
            stream = (_post(stream, mod, ng, (of, ob, r), weights, gla=True, latent_only=last),)
    return stream[0]
```

```python
import functools
import math
from typing import NamedTuple

import jax
import jax.numpy as jnp
from jax import lax
from jax.experimental import pallas as pl
from jax.experimental.pallas import tpu as pltpu

N_MOD = 6
NORM_EPS = 1e-6
GRID_W = 64
MLA_HEADS = 8
MLA_NOPE = 128
MLA_ROPE = 64
MLA_V = 128
MLA_VX = MLA_V + 16
MLA_Q_RANK = 384
MLA_KV_RANK = 256
ROPE_BASE = 10000.0
GLA_HEADS = 4
GLA_RANK = 16
GLA_TAU = 16.0
GLA_CHUNK = 64

LANES = 128
TILE = 256
MLA_QK = 2 * LANES
GLA_DECAY_ROWS = 8
MIB = 1024 * 1024
LOG2E = 1.4426950408889634

F32 = jnp.float32
BF16 = jnp.bfloat16
NT = (((1,), (1,)), ((), ()))


def _dot(a, b):
    return jnp.dot(a, b, preferred_element_type=F32)


def _dot_nt(a, b):
    return lax.dot_general(a, b, NT, preferred_element_type=F32)


def _rms(x, g):
    ms = jnp.mean(x * x, axis=-1, keepdims=True)
    return x * lax.rsqrt(ms + NORM_EPS) * g


class _LayerParam(NamedTuple):
    stack: jax.Array
    layer: int


def _param_spec(p):
    zeros = (0,) * (p.stack.ndim - 1)
    return pl.BlockSpec((None,) + p.stack.shape[1:], lambda *_: (p.layer,) + zeros,
                        pipeline_mode=pl.Buffered(1))


def _mod_spec(mod, group, n_tiles, n_batch):
    d = mod.stack.shape[-1]
    return pl.BlockSpec((None, group, N_MOD, d),
                        lambda bi, i: (mod.layer, jnp.where(i == n_tiles - 1, n_batch // group, bi), 0, 0))


def _params(sem, vmem_mib):
    return pltpu.CompilerParams(dimension_semantics=sem, vmem_limit_bytes=vmem_mib * MIB)


def _stream_specs(stream, n_tiles, d, nb=1):
    if len(stream) == 1:
        return [pl.BlockSpec((nb, TILE, d), lambda bi, i: (bi, i, 0))]
    last_lat = n_tiles - 2
    return [pl.BlockSpec((nb, TILE, d), lambda bi, i: (bi, jnp.minimum(i, last_lat), 0)),
            pl.BlockSpec((nb, TILE, d), lambda bi, i: (bi, 0, 0))]


def _stream_tile(stream_refs, u=0):
    if len(stream_refs) == 1:
        return stream_refs[0][u]
    is_ctx = pl.program_id(1) == pl.num_programs(1) - 1
    return jnp.where(is_ctx, stream_refs[1][u], stream_refs[0][u])


def _batch_group(b):
    return 2 if b % 2 == 0 else 1


def _ada_kernel(c_ref, w_ref, b_ref, o_ref):
    c = c_ref[...]
    a = c / (1.0 + jnp.exp(-c))
    a_hi = a.astype(BF16)
    a_lo = (a - a_hi.astype(F32)).astype(BF16)
    w = w_ref[0]
    w_hi = w.astype(BF16)
    w_lo = (w - w_hi.astype(F32)).astype(BF16)
    o_ref[0] = _dot(a_hi, w_hi) + _dot(a_lo, w_hi) + _dot(a_hi, w_lo) + b_ref[0]


def _ada(cvec, ada_w, ada_b):
    depth, d, n = ada_w.shape
    rows = cvec.shape[0]
    tn = 1536
    return pl.pallas_call(
        _ada_kernel,
        out_shape=jax.ShapeDtypeStruct((depth, rows, n), F32),
        grid=(depth, n // tn),
        in_specs=[
            pl.BlockSpec((rows, d), lambda l, j: (0, 0)),
            pl.BlockSpec((1, d, tn), lambda l, j: (l, 0, j)),
            pl.BlockSpec((1, 1, tn), lambda l, j: (l, 0, j)),
        ],
        out_specs=pl.BlockSpec((1, rows, tn), lambda l, j: (l, 0, j)),
        compiler_params=_params(("parallel", "parallel"), 40),
        name="ada",
    )(cvec, ada_w, ada_b.reshape(depth, 1, n))


def _mla_proj_kernel(*refs, n_stream, q_scale):
    (mod_ref, ng_ref, wd_ref, gq_ref, gkv_ref, wqt_ref, wk_ref, wvt_ref,
     rcs_ref, rcst_ref, qt_ref, k_ref, vt_ref) = refs[n_stream:]
    stream_refs = refs[:n_stream]
    group = range(qt_ref.shape[0])
    nq, nkv, nr = MLA_Q_RANK, MLA_KV_RANK, MLA_ROPE
    hw = MLA_HEADS * LANES
    hr = MLA_HEADS * nr

    def down(u):
        mod = mod_ref[u]
        h = (_rms(_stream_tile(stream_refs, u), ng_ref[0:1]) * (1.0 + mod[1:2]) + mod[0:1]).astype(BF16)
        return _dot(h, wd_ref[...])

    def latents(a):
        cq = _rms(a[:, :nq], gq_ref[...]).astype(BF16)
        ckv = _rms(a[:, nq:nq + nkv], gkv_ref[...]).astype(BF16)
        y = a[:, nq + nkv:] * rcs_ref[...]
        lane = lax.broadcasted_iota(jnp.int32, y.shape, 1)
        k_rope = jnp.where(lane < nr, y + pltpu.roll(y, nr, axis=1), 0.0).astype(BF16)
        return cq, ckv, k_rope

    def up(cq, ckv):
        qt = _dot_nt(wqt_ref[...], cq) * q_scale
        k_nope = _dot(ckv, wk_ref[...]).astype(BF16)
        vt = _dot_nt(wvt_ref[...], ckv).astype(BF16)
        return qt, k_nope, vt

    def emit(u, k_rope, qt, k_nope, vt):
        cos_t = rcst_ref[0:nr]
        sin_t = rcst_ref[nr:2 * nr]
        for hd in range(MLA_HEADS):
            s = hd * LANES
            r = hw + hd * nr
            qt_ref[u, hd, 0:LANES, :] = qt[s:s + LANES].astype(BF16)
            q_rope = qt[r:r + nr] * cos_t + qt[r + hr:r + hr + nr] * sin_t
            qt_ref[u, hd, LANES:LANES + nr, :] = q_rope.astype(BF16)
            qt_ref[u, hd, LANES + nr:, :] = jnp.zeros((MLA_QK - LANES - nr, TILE), BF16)
            k_ref[u, hd, :, 0:LANES] = k_nope[:, s:s + LANES]
            k_ref[u, hd, :, LANES:2 * LANES] = k_rope
            vt_ref[u, hd, 0:MLA_V] = vt[s:s + LANES]
            vt_ref[u, hd, MLA_V:MLA_VX] = jnp.ones((MLA_VX - MLA_V, TILE), BF16)

    downs = [down(u) for u in group]
    lats = [latents(downs[u]) for u in group]
    ups = [up(lats[u][0], lats[u][1]) for u in group]
    for u in group:
        emit(u, lats[u][2], *ups[u])


def _mla_proj(stream, mod, params, rcs, rcst):
    b, _, d = stream[0].shape
    t = sum(a.shape[1] for a in stream)
    nt = t // TILE
    hds = MLA_HEADS
    g = _batch_group(b)
    q_scale = (MLA_NOPE + MLA_ROPE) ** -0.5 * LOG2E
    return pl.pallas_call(
        functools.partial(_mla_proj_kernel, n_stream=len(stream), q_scale=q_scale),
        out_shape=(
            jax.ShapeDtypeStruct((b, hds, MLA_QK, t), BF16),
            jax.ShapeDtypeStruct((b, hds, t, MLA_QK), BF16),
            jax.ShapeDtypeStruct((b, hds, MLA_VX, t), BF16),
        ),
        grid=(b // g, nt),
        in_specs=_stream_specs(stream, nt, d, g) + [_mod_spec(mod, g, nt, b)]
        + [_param_spec(p) for p in params] + [
            pl.BlockSpec((TILE, LANES), lambda bi, i: (i, 0)),
            pl.BlockSpec((LANES, TILE), lambda bi, i: (0, i)),
        ],
        out_specs=(
            pl.BlockSpec((g, hds, MLA_QK, TILE), lambda bi, i: (bi, 0, 0, i)),
            pl.BlockSpec((g, hds, TILE, MLA_QK), lambda bi, i: (bi, 0, i, 0)),
            pl.BlockSpec((g, hds, MLA_VX, TILE), lambda bi, i: (bi, 0, 0, i)),
        ),
        compiler_params=_params(("parallel", "parallel"), 48),
        name="mla_proj",
    )(*stream, mod.stack, *[p.stack for p in params], rcs, rcst)


def _attn_kernel(qt_ref, k_ref, vt_ref, o_ref, s_ref, p_ref, *, tq, tk, n_tiles, n_ctx):
    n_keys = k_ref.shape[2]
    n_chunks = n_keys // tk
    assert n_chunks >= 2

    def s_slot(j):
        return 2 if j == 0 else j % 2

    def q_start(i):
        return i * tq if isinstance(i, int) else pl.multiple_of(i * tq, tq)

    def scores(i, j):
        qt = qt_ref[0, 0, :, pl.ds(q_start(i), tq)]
        k = k_ref[0, 0, j * tk:(j + 1) * tk, :]
        s = _dot(k, qt)
        s_ref[s_slot(j)] = s
        return jnp.max(s, axis=0, keepdims=True)

    def softmax(j, m, s_max):
        m_new = jnp.maximum(m, s_max)
        p = jnp.exp2(s_ref[s_slot(j)] - m_new)
        p_ref[j % 2] = p.astype(BF16)
        return m_new, jnp.exp2(m - m_new)

    def pv(j, acc, alpha):
        vt = vt_ref[0, 0, :, j * tk:(j + 1) * tk]
        return alpha * acc + _dot(vt, p_ref[j % 2])

    def tile(i, s_max):
        m = jnp.full((1, tq), -jnp.inf, F32)
        acc = jnp.zeros((MLA_VX, tq), F32)
        alpha = None
        for j in range(n_chunks):
            if j > 0:
                acc = pv(j - 1, acc, alpha)
            if j + 1 < n_chunks:
                next_max = scores(i, j + 1)
            else:
                next_max = scores(jnp.minimum(i + 1, n_tiles - 1), 0)
            m, alpha = softmax(j, m, s_max)
            s_max = next_max
        acc = pv(n_chunks - 1, acc, alpha)
        o = acc[:MLA_V] * (1.0 / acc[MLA_V:MLA_V + 1])
        o_ref[0, pl.ds(q_start(i), tq), :] = o.T.astype(o_ref.dtype)
        return s_max

    lax.fori_loop(0, n_tiles, tile, scores(0, 0), unroll=math.gcd(n_tiles, 4))

    c0 = n_keys - n_ctx
    s = _dot(k_ref[0, 0, c0:, :], qt_ref[0, 0, :, c0:])
    p = jnp.exp2(s - jnp.max(s, axis=0, keepdims=True)).astype(BF16)
    acc = _dot(vt_ref[0, 0, :, c0:], p)
    o = acc[:MLA_V] * (1.0 / acc[MLA_V:MLA_V + 1])
    o_ref[0, c0:, :] = o.T.astype(o_ref.dtype)


def _attn_tk(n_keys):
    for tk in (768, 512, 256):
        if n_keys % tk == 0:
            return tk
    raise ValueError(f"unsupported key count {n_keys}")


def _mla_attention(qt, k, vt, n_lat, tq):
    b, hds, _, t = qt.shape
    tk = _attn_tk(t)
    return pl.pallas_call(
        functools.partial(_attn_kernel, tq=tq, tk=tk, n_tiles=n_lat // tq, n_ctx=t - n_lat),
        out_shape=jax.ShapeDtypeStruct((b, t, hds * MLA_V), BF16),
        grid=(b, hds),
        in_specs=[
            pl.BlockSpec((1, 1, MLA_QK, t), lambda bi, h: (bi, h, 0, 0)),
            pl.BlockSpec((1, 1, t, MLA_QK), lambda bi, h: (bi, h, 0, 0)),
            pl.BlockSpec((1, 1, MLA_VX, t), lambda bi, h: (bi, h, 0, 0)),
        ],
        out_specs=pl.BlockSpec((1, t, MLA_V), lambda bi, h: (bi, 0, h)),
        scratch_shapes=[pltpu.VMEM((3, tk, tq), F32), pltpu.VMEM((2, tk, tq), BF16)],
        compiler_params=_params(("parallel", "parallel"), 56),
        name="mla_attn",
    )(qt, k, vt)


def _gla_proj_kernel(x_ref, mod_ref, ng_ref, wg_ref, w2_ref, bg_ref,
                     qf_ref, kf_ref, kfe_ref, qb_ref, kb_ref, kbe_ref, v_ref, r_ref, d_ref):
    group = range(x_ref.shape[0])
    dk = GLA_HEADS * LANES
    dv = x_ref.shape[2]
    n_chunks = TILE // GLA_CHUNK

    def project(u):
        mod = mod_ref[u]
        h = (_rms(x_ref[u], ng_ref[0:1]) * (1.0 + mod[1:2]) + mod[0:1]).astype(BF16)
        return _dot(h, wg_ref[...])

    def gates(u, big):
        v_ref[u] = big[:, 2 * dk:2 * dk + dv].astype(BF16)
        rr = big[:, 2 * dk + dv:2 * dk + 2 * dv]
        r_ref[u] = (rr / (1.0 + jnp.exp(-rr))).astype(BF16)
        low = big[:, 2 * dk + 2 * dv:].astype(BF16)
        z = _dot(low, w2_ref[...]) + bg_ref[...]
        g = -(jnp.maximum(-z, 0.0) + jnp.log(1.0 + jnp.exp(-jnp.abs(z)))) * (1.0 / GLA_TAU)
        g_hi = g.astype(BF16)
        return g_hi, (g - g_hi.astype(F32)).astype(BF16)

    row = lax.broadcasted_iota(jnp.int32, (TILE, TILE), 0)
    col = lax.broadcasted_iota(jnp.int32, (TILE, TILE), 1)
    shift = GLA_CHUNK.bit_length() - 1
    same = (row >> shift) == (col >> shift)
    prefix = jnp.where(same & (col <= row), 1.0, 0.0).astype(BF16)
    suffix = jnp.where(same & (col >= row), 1.0, 0.0).astype(BF16)

    def cumsums(g_hi, g_lo):
        gf = _dot(prefix, g_hi[:, :dk]) + _dot(prefix, g_lo[:, :dk])
        gb = _dot(suffix, g_hi[:, dk:]) + _dot(suffix, g_lo[:, dk:])
        return gf, gb

    def emit(u, big, gf, gb):
        q = big[:, :dk] * (LANES ** -0.5)
        k = big[:, dk:2 * dk]
        tot_f = [gf[(c + 1) * GLA_CHUNK - 1:(c + 1) * GLA_CHUNK] for c in range(n_chunks)]
        tot_b = [gb[c * GLA_CHUNK:c * GLA_CHUNK + 1] for c in range(n_chunks)]
        dec_f = [jnp.exp(t) for t in tot_f]
        dec_b = [jnp.exp(t) for t in tot_b]
        for c in range(n_chunks):
            d_ref[u, 0, c:c + 1, 0:dk] = dec_f[c]
            d_ref[u, 0, c:c + 1, dk:2 * dk] = dec_b[c]
        d_ref[u, 0, n_chunks:, :] = jnp.ones((d_ref.shape[2] - n_chunks, 2 * dk), F32)
        kf = k * jnp.exp(-gf)
        kb = k * jnp.exp(-gb)
        qf_ref[u] = (q * jnp.exp(gf)).astype(BF16)
        kf_ref[u] = kf.astype(BF16)
        kfe_ref[u] = (kf * jnp.concatenate([jnp.broadcast_to(t, (GLA_CHUNK, dk)) for t in dec_f], axis=0)).astype(BF16)
        qb_ref[u] = (q * jnp.exp(gb)).astype(BF16)
        kb_ref[u] = kb.astype(BF16)
        kbe_ref[u] = (kb * jnp.concatenate([jnp.broadcast_to(t, (GLA_CHUNK, dk)) for t in dec_b], axis=0)).astype(BF16)

    pending = None
    for u in group:
        big = project(u)
        if pending is not None:
            emit(pending[0], pending[1], *cumsums(*pending[2]))
        pending = (u, big, gates(u, big))
    emit(pending[0], pending[1], *cumsums(*pending[2]))


def _gla_proj(xa, mod, params):
    b, t, d = xa.shape
    nt = t // TILE
    dk = GLA_HEADS * LANES
    g = _batch_group(b)
    row_spec = lambda w: pl.BlockSpec((g, TILE, w), lambda bi, i: (bi, i, 0))
    qk = jax.ShapeDtypeStruct((b, t, dk), BF16)
    wide = jax.ShapeDtypeStruct((b, t, d), BF16)
    return pl.pallas_call(
        _gla_proj_kernel,
        out_shape=(qk, qk, qk, qk, qk, qk, wide, wide,
                   jax.ShapeDtypeStruct((b, nt, GLA_DECAY_ROWS, 2 * dk), F32)),
        grid=(b // g, nt),
        in_specs=[row_spec(d), _mod_spec(mod, g, nt, b)] + [_param_spec(p) for p in params],
        out_specs=(row_spec(dk),) * 6 + (
            row_spec(d),
            row_spec(d),
            pl.BlockSpec((g, 1, GLA_DECAY_ROWS, 2 * dk), lambda bi, i: (bi, i, 0, 0)),
        ),
        compiler_params=_params(("parallel", "parallel"), 56),
        name="gla_proj",
    )(xa, mod.stack, *[p.stack for p in params])


def _gla_scan_kernel(qf_ref, kf_ref, kfe_ref, vf_ref, df_ref,
                     qb_ref, kb_ref, kbe_ref, vb_ref, db_ref,
                     of_ref, ob_ref, s_ref):
    @pl.when(pl.program_id(1) == 0)
    def _():
        s_ref[...] = jnp.zeros(s_ref.shape, s_ref.dtype)

    L = GLA_CHUNK
    n_chunks = TILE // L
    dk = LANES
    dv = vf_ref.shape[2] // GLA_HEADS
    shift = L.bit_length() - 1
    ri = lax.broadcasted_iota(jnp.int32, (TILE, TILE), 0)
    ci = lax.broadcasted_iota(jnp.int32, (TILE, TILE), 1)
    same_chunk = (ri >> shift) == (ci >> shift)
    causal = same_chunk & (ci <= ri)
    anti = same_chunk & (ci >= ri)
    token_chunk = lax.broadcasted_iota(jnp.int32, (dk, TILE), 1) >> shift

    directions = (
        (qf_ref, kf_ref, kfe_ref, vf_ref, df_ref, of_ref, causal, range(n_chunks)),
        (qb_ref, kb_ref, kbe_ref, vb_ref, db_ref, ob_ref, anti, range(n_chunks - 1, -1, -1)),
    )
    units = [(u, direction, hd) for u in range(qf_ref.shape[0]) for direction in range(2)
             for hd in range(GLA_HEADS)]

    def intra(u, direction, hd):
        q_ref, k_ref, _, _, _, _, mask, _ = directions[direction]
        kc = slice(hd * dk, (hd + 1) * dk)
        return jnp.where(mask, _dot_nt(q_ref[u, :, kc], k_ref[u, :, kc]), 0.0).astype(BF16)

    def values(u, direction, hd, a):
        _, _, ke_ref, v_ref, _, _, _, _ = directions[direction]
        kt = ke_ref[u, :, hd * dk:(hd + 1) * dk].astype(F32).T
        kt_blocks = [jnp.where(token_chunk == c, kt, 0.0).astype(BF16) for c in range(n_chunks)]
        return _dot(jnp.concatenate([a] + kt_blocks, axis=0), v_ref[u, :, hd * dv:(hd + 1) * dv])

    def recur(u, direction, hd, r):
        q_ref, _, _, _, d_ref, o_ref, _, order = directions[direction]
        col = direction * GLA_HEADS + hd
        idx = u * 2 * GLA_HEADS + col
        d_rows = d_ref[u, 0, :, col * dk:(col + 1) * dk]
        d_cols = jnp.concatenate([d_rows, jnp.zeros((dk - d_rows.shape[0], dk), F32)], axis=0).T
        st = s_ref[idx]
        for c in order:
            rows = slice(c * L, (c + 1) * L)
            o = r[rows] + _dot(q_ref[u, rows, hd * dk:(hd + 1) * dk], st.astype(BF16))
            o_ref[u, rows, hd * dv:(hd + 1) * dv] = o.astype(o_ref.dtype)
            st = st * d_cols[:, c:c + 1] + r[TILE + c * dk:TILE + (c + 1) * dk]
        s_ref[idx] = st

    a_all = [intra(*u) for u in units]
    r_all = [values(*u, a) for u, a in zip(units, a_all)]
    for u, r in zip(units, r_all):
        recur(*u, r)


def _gla_scan(qf, kf, kfe, qb, kb, kbe, v, dd):
    b, t, dkh = qf.shape
    d = v.shape[2]
    nt = t // TILE
    f_tile = lambda s: jnp.where(s == 0, nt - 1, s - 1)
    b_tile = lambda s: jnp.where(s == 0, nt - 1, nt - 1 - s)

    g = _batch_group(b)

    def specs(tile):
        row = lambda w: pl.BlockSpec((g, TILE, w), lambda bi, s: (bi, tile(s), 0))
        return [row(dkh), row(dkh), row(dkh), row(d),
                pl.BlockSpec((g, 1, GLA_DECAY_ROWS, 2 * dkh), lambda bi, s: (bi, tile(s), 0, 0))]

    out = jax.ShapeDtypeStruct((b, t, d), BF16)
    return pl.pallas_call(
        _gla_scan_kernel,
        out_shape=(out, out),
        grid=(b // g, nt),
        in_specs=specs(f_tile) + specs(b_tile),
        out_specs=(pl.BlockSpec((g, TILE, d), lambda bi, s: (bi, f_tile(s), 0)),
                   pl.BlockSpec((g, TILE, d), lambda bi, s: (bi, b_tile(s), 0))),
        scratch_shapes=[pltpu.VMEM((g * 2 * GLA_HEADS, LANES, d // GLA_HEADS), F32)],
        compiler_params=_params(("parallel", "arbitrary"), 48),
        name="gla_scan",
    )(qf, kf, kfe, v, dd, qb, kb, kbe, v, dd)


def _post_kernel(*refs, n_stream, gla):
    stream_refs = refs[:n_stream]
    refs = refs[n_stream:]
    if gla:
        mod_ref, ng_ref, of_ref, ob_ref, r_ref, go_ref, wo_ref, w1_ref, w2_ref, out_ref = refs
    else:
        mod_ref, ng_ref, o_ref, wo_ref, w1_ref, w2_ref, out_ref = refs
    ng = ng_ref[...]
    group = range(out_ref.shape[0])
    d = out_ref.shape[2]

    def mixer_proj(u):
        if gla:
            o = of_ref[u].astype(F32) + ob_ref[u].astype(F32)
            dv = d // GLA_HEADS
            go = go_ref[...]
            o = jnp.concatenate([_rms(o[:, i * dv:(i + 1) * dv], go) for i in range(GLA_HEADS)], axis=1)
            y_in = (o * r_ref[u].astype(F32)).astype(BF16)
        else:
            y_in = o_ref[u]
        return _dot(y_in, wo_ref[...])

    def residual(u, y):
        mod = mod_ref[u]
        x1 = _stream_tile(stream_refs, u) + mod[2:3] * _rms(y, ng[1:2])
        return x1, (_rms(x1, ng[2:3]) * (1.0 + mod[4:5]) + mod[3:4]).astype(BF16)

    def mlp(h):
        z = jnp.zeros((TILE, d), F32)
        for j in range(w1_ref.shape[1] // d):
            a = jnp.maximum(_dot(h, w1_ref[:, j * d:(j + 1) * d]), 0.0)
            z = z + _dot((a * a).astype(BF16), w2_ref[j * d:(j + 1) * d, :])
        return z

    ys = [mixer_proj(u) for u in group]
    zs, x1s = [], []
    for u in group:
        x1, h = residual(u, ys[u])
        x1s.append(x1)
        zs.append(mlp(h))
    for u in group:
        out_ref[u] = x1s[u] + mod_ref[u][5:6] * _rms(zs[u], ng[3:4])


def _post(stream, mod, ng, mixer_out, weights, gla, latent_only):
    b, _, d = stream[0].shape
    t = mixer_out[0].shape[1]
    nt = t // TILE
    if latent_only:
        stream = stream[:1]
    n_out = nt - 1 if latent_only else nt
    g = _batch_group(b)
    row = pl.BlockSpec((g, TILE, d), lambda bi, i: (bi, i, 0))
    in_place = len(stream) == 1 and not latent_only
    return pl.pallas_call(
        functools.partial(_post_kernel, n_stream=len(stream), gla=gla),
        out_shape=jax.ShapeDtypeStruct((b, n_out * TILE, d), stream[0].dtype),
        grid=(b // g, n_out),
        in_specs=_stream_specs(stream, nt, d, g)
        + [_mod_spec(mod, g, nt, b), _param_spec(ng)]
        + [row] * len(mixer_out) + [_param_spec(w) for w in weights],
        out_specs=row,
        input_output_aliases={0: 0} if in_place else {},
        compiler_params=_params(("parallel", "parallel"), 56),
        name="gla_post" if gla else "mla_post",
    )(*stream, mod.stack, ng.stack, *mixer_out, *[w.stack for w in weights])


def _rope_tables(n_lat, n_ctx):
    t = jnp.arange(n_lat)
    row = (t // GRID_W).astype(F32)
    col = (t % GRID_W).astype(F32)
    half = MLA_ROPE // 2
    inv_freq = ROPE_BASE ** (-jnp.arange(0, half, 2, dtype=F32) / half)
    ang_r = row[:, None] * inv_freq
    ang_c = col[:, None] * inv_freq
    cr, sr, cc, sc = jnp.cos(ang_r), jnp.sin(ang_r), jnp.cos(ang_c), jnp.sin(ang_c)
    cos = jnp.concatenate([cr, cr, cc, cc], axis=1)
    sin = jnp.concatenate([-sr, sr, -sc, sc], axis=1)
    cos = jnp.concatenate([cos, jnp.ones((n_ctx, MLA_ROPE), F32)], axis=0)
    sin = jnp.concatenate([sin, jnp.zeros((n_ctx, MLA_ROPE), F32)], axis=0)
    assert 2 * MLA_ROPE == LANES
    return jnp.concatenate([cos, sin], axis=1)


def _swap_perm():
    q = MLA_ROPE // 4
    return jnp.array(list(range(q, 2 * q)) + list(range(0, q)) + list(range(3 * q, 4 * q)) + list(range(2 * q, 3 * q)))


def _mla_weights(w_dq, g_q, w_uq, w_dkv, g_kv, w_ukv, w_o):
    n = w_dq.shape[0]
    perm = _swap_perm()
    rope = w_dkv[:, :, MLA_KV_RANK:]
    wd = jnp.concatenate([w_dq, w_dkv[:, :, :MLA_KV_RANK], rope, rope[:, :, perm]], axis=2)
    uq = w_uq.reshape(n, MLA_Q_RANK, MLA_HEADS, MLA_NOPE + MLA_ROPE)
    q_nope = uq[..., :MLA_NOPE].reshape(n, MLA_Q_RANK, -1)
    q_rope = uq[..., MLA_NOPE:]
    q_swap = q_rope[..., perm]
    wq = jnp.concatenate([q_nope, q_rope.reshape(n, MLA_Q_RANK, -1), q_swap.reshape(n, MLA_Q_RANK, -1)], axis=2)
    ukv = w_ukv.reshape(n, MLA_KV_RANK, MLA_HEADS, MLA_NOPE + MLA_V)
    wk = ukv[..., :MLA_NOPE].reshape(n, MLA_KV_RANK, -1)
    wv = ukv[..., MLA_NOPE:].reshape(n, MLA_KV_RANK, -1)
    proj = (wd.astype(BF16), g_q[:, None, :], g_kv[:, None, :], wq.astype(BF16).transpose(0, 2, 1),
            wk.astype(BF16), wv.astype(BF16).transpose(0, 2, 1))
    return proj, w_o.astype(BF16)


def _gla_weights(w_q, w_k, w_v, w_r, w_gate1, w_gate2, b_gate, g_o, w_o):
    n, d, dk = w_q.shape
    low_pad = jnp.zeros((n, d, LANES - 2 * GLA_RANK), w_q.dtype)
    wg = jnp.concatenate([w_q, w_k, w_v, w_r, w_gate1[:, 0], w_gate1[:, 1], low_pad], axis=2)
    zero = jnp.zeros((n, GLA_RANK, dk), w_gate2.dtype)
    w2 = jnp.concatenate([
        jnp.concatenate([w_gate2[:, 0], zero], axis=2),
        jnp.concatenate([zero, w_gate2[:, 1]], axis=2),
        jnp.zeros((n, LANES - 2 * GLA_RANK, 2 * dk), w_gate2.dtype)], axis=1)
    bg = jnp.concatenate([b_gate[:, 0], b_gate[:, 1]], axis=1)[:, None, :]
    return (wg.astype(BF16), w2.astype(BF16), bg), g_o[:, None, :], w_o.astype(BF16)


def kernel(x, c, ctx, c_ctx, ada_w, ada_b, norm_g, mlp_w1, mlp_w2, mla_w_dq, mla_g_q, mla_w_uq, mla_w_dkv, mla_g_kv, mla_w_ukv, mla_w_o, gla_w_q, gla_w_k, gla_w_v, gla_w_r, gla_w_gate1, gla_w_gate2, gla_b_gate, gla_g_o, gla_w_o):
    b, s, d = x.shape
    n_ctx = ctx.shape[1]
    depth = ada_w.shape[0]
    assert n_ctx == TILE and s % TILE == 0 and d == GLA_HEADS * 2 * LANES

    stream = (x, ctx)
    g = _batch_group(b)
    rows = -(-(b + g) // 16) * 16
    cvec = jnp.concatenate([c, jnp.tile(c_ctx[None], (g, 1)), jnp.zeros((rows - b - g, d), c.dtype)], axis=0)
    mod_all = _ada(cvec, ada_w, ada_b).reshape(depth, rows, N_MOD, d)

    rcs = _rope_tables(s, n_ctx)
    rcst = rcs.T
    tq = 512 if s % 512 == 0 else TILE

    w1_all = mlp_w1.astype(BF16)
    w2_all = mlp_w2.astype(BF16)
    mla_proj_w, mla_wo = _mla_weights(mla_w_dq, mla_g_q, mla_w_uq, mla_w_dkv, mla_g_kv, mla_w_ukv, mla_w_o)
    gla_proj_w, gla_go, gla_wo = _gla_weights(gla_w_q, gla_w_k, gla_w_v, gla_w_r, gla_w_gate1, gla_w_gate2,
                                              gla_b_gate, gla_g_o, gla_w_o)

    for i in range(depth):
        j = i // 2
        mod = _LayerParam(mod_all, i)
        ng = _LayerParam(norm_g, i)
        mlp = (_LayerParam(w1_all, i), _LayerParam(w2_all, i))
        last = i == depth - 1
        if i % 2 == 0:
            qt, k, vt = _mla_proj(stream, mod, [ng] + [_LayerParam(w, j) for w in mla_proj_w], rcs, rcst)
            o = _mla_attention(qt, k, vt, s, tq)
            weights = (_LayerParam(mla_wo, j),) + mlp
            stream = (_post(stream, mod, ng, (o,), weights, gla=False, latent_only=last),)
        else:
            (xa,) = stream
            qf, kf, kfe, qb, kb, kbe, v, r, dd = _gla_proj(xa, mod, [ng] + [_LayerParam(w, j) for w in gla_proj_w])
            of, ob = _gla_scan(qf, kf, kfe, qb, kb, kbe, v, dd)
            weights = (_LayerParam(gla_go, j), _LayerParam(gla_wo, j)) + mlp
            stream = (_post(stream, mod, ng, (of, ob, r), weights, gla=True, latent_only=last),)
    return stream[0]
```

```python
import functools
import math
from typing import NamedTuple

import jax
import jax.numpy as jnp
from jax import lax
from jax.experimental import pallas as pl
from jax.experimental.pallas import tpu as pltpu

N_MOD = 6
NORM_EPS = 1e-6
GRID_W = 64
MLA_HEADS = 8
MLA_NOPE = 128
MLA_ROPE = 64
MLA_V = 128
MLA_VX = MLA_V + 16
MLA_Q_RANK = 384
MLA_KV_RANK = 256
ROPE_BASE = 10000.0
GLA_HEADS = 4
GLA_RANK = 16
GLA_TAU = 16.0
GLA_CHUNK = 64

LANES = 128
TILE = 256
MLA_QK = 2 * LANES
GLA_DECAY_ROWS = 8
ADA_COLS = 1536
MIB = 1024 * 1024
LOG2E = 1.4426950408889634

F32 = jnp.float32
BF16 = jnp.bfloat16
NT = (((1,), (1,)), ((), ()))


def _dot(a, b):
    return jnp.dot(a, b, preferred_element_type=F32)


def _dot_nt(a, b):
    return lax.dot_general(a, b, NT, preferred_element_type=F32)


def _rms(x, g):
    ms = jnp.mean(x * x, axis=-1, keepdims=True)
    return x * lax.rsqrt(ms + NORM_EPS) * g


class _LayerParam(NamedTuple):
    stack: jax.Array
    layer: int


def _param_spec(p):
    zeros = (0,) * (p.stack.ndim - 1)
    return pl.BlockSpec((None,) + p.stack.shape[1:], lambda *_: (p.layer,) + zeros,
                        pipeline_mode=pl.Buffered(1))


def _mod_spec(mod, group, n_tiles, n_batch):
    d = mod.stack.shape[-1]
    return pl.BlockSpec((None, group, N_MOD, d),
                        lambda bi, i: (mod.layer, jnp.where(i == n_tiles - 1, n_batch // group, bi), 0, 0))


VMEM_MIB = {
    "ada": 40,
    "mla_proj": 48,
    "mla_attn": 56,
    "gla_proj": 56,
    "gla_scan": 48,
    "post": 56,
}


def _params(name, sem):
    return pltpu.CompilerParams(dimension_semantics=sem, vmem_limit_bytes=VMEM_MIB[name] * MIB)


def _stream_specs(stream, n_tiles, d, nb=1):
    if len(stream) == 1:
        return [pl.BlockSpec((nb, TILE, d), lambda bi, i: (bi, i, 0))]
    last_lat = n_tiles - 2
    return [pl.BlockSpec((nb, TILE, d), lambda bi, i: (bi, jnp.minimum(i, last_lat), 0)),
            pl.BlockSpec((nb, TILE, d), lambda bi, i: (bi, 0, 0))]


def _stream_tile(stream_refs, u=0):
    if len(stream_refs) == 1:
        return stream_refs[0][u]
    is_ctx = pl.program_id(1) == pl.num_programs(1) - 1
    return jnp.where(is_ctx, stream_refs[1][u], stream_refs[0][u])


MAX_BATCH_GROUP = 4


def _batch_group(b, limit=2):
    g = limit
    while b % g:
        g //= 2
    return g


def _ada_kernel(c_ref, w_ref, b_ref, o_ref):
    c = c_ref[...]
    a = c / (1.0 + jnp.exp(-c))
    a_hi = a.astype(BF16)
    a_lo = (a - a_hi.astype(F32)).astype(BF16)
    w = w_ref[0]
    w_hi = w.astype(BF16)
    w_lo = (w - w_hi.astype(F32)).astype(BF16)
    o_ref[0] = _dot(a_hi, w_hi) + _dot(a_lo, w_hi) + _dot(a_hi, w_lo) + b_ref[0]


def _ada(cvec, ada_w, ada_b):
    depth, d, n = ada_w.shape
    rows = cvec.shape[0]
    tn = ADA_COLS
    assert n % tn == 0
    return pl.pallas_call(
        _ada_kernel,
        out_shape=jax.ShapeDtypeStruct((depth, rows, n), F32),
        grid=(depth, n // tn),
        in_specs=[
            pl.BlockSpec((rows, d), lambda l, j: (0, 0)),
            pl.BlockSpec((1, d, tn), lambda l, j: (l, 0, j)),
            pl.BlockSpec((1, 1, tn), lambda l, j: (l, 0, j)),
        ],
        out_specs=pl.BlockSpec((1, rows, tn), lambda l, j: (l, 0, j)),
        compiler_params=_params("ada", ("parallel", "parallel")),
        name="ada",
    )(cvec, ada_w, ada_b.reshape(depth, 1, n))


def _mla_proj_kernel(*refs, n_stream, q_scale):
    (mod_ref, ng_ref, wd_ref, gq_ref, gkv_ref, wqt_ref, wk_ref, wvt_ref,
     rcs_ref, rcst_ref, qt_ref, k_ref, vt_ref) = refs[n_stream:]
    stream_refs = refs[:n_stream]
    group = range(qt_ref.shape[0])
    nq, nkv, nr = MLA_Q_RANK, MLA_KV_RANK, MLA_ROPE
    hw = MLA_HEADS * LANES
    hr = MLA_HEADS * nr

    def down(u):
        mod = mod_ref[u]
        h = (_rms(_stream_tile(stream_refs, u), ng_ref[0:1]) * (1.0 + mod[1:2]) + mod[0:1]).astype(BF16)
        return _dot(h, wd_ref[...])

    def latents(a):
        cq = _rms(a[:, :nq], gq_ref[...]).astype(BF16)
        ckv = _rms(a[:, nq:nq + nkv], gkv_ref[...]).astype(BF16)
        y = a[:, nq + nkv:] * rcs_ref[...]
        lane = lax.broadcasted_iota(jnp.int32, y.shape, 1)
        k_rope = jnp.where(lane < nr, y + pltpu.roll(y, nr, axis=1), 0.0).astype(BF16)
        return cq, ckv, k_rope

    def up(cq, ckv):
        qt = _dot_nt(wqt_ref[...], cq) * q_scale
        k_nope = _dot(ckv, wk_ref[...]).astype(BF16)
        vt = _dot_nt(wvt_ref[...], ckv).astype(BF16)
        return qt, k_nope, vt

    def emit(u, k_rope, qt, k_nope, vt):
        cos_t = rcst_ref[0:nr]
        sin_t = rcst_ref[nr:2 * nr]
        for hd in range(MLA_HEADS):
            s = hd * LANES
            r = hw + hd * nr
            qt_ref[u, hd, 0:LANES, :] = qt[s:s + LANES].astype(BF16)
            q_rope = qt[r:r + nr] * cos_t + qt[r + hr:r + hr + nr] * sin_t
            qt_ref[u, hd, LANES:LANES + nr, :] = q_rope.astype(BF16)
            qt_ref[u, hd, LANES + nr:, :] = jnp.zeros((MLA_QK - LANES - nr, TILE), BF16)
            k_ref[u, hd, :, 0:LANES] = k_nope[:, s:s + LANES]
            k_ref[u, hd, :, LANES:2 * LANES] = k_rope
            vt_ref[u, hd, 0:MLA_V] = vt[s:s + LANES]
            vt_ref[u, hd, MLA_V:MLA_VX] = jnp.ones((MLA_VX - MLA_V, TILE), BF16)

    downs = [down(u) for u in group]
    lats = [latents(downs[u]) for u in group]
    ups = [up(lats[u][0], lats[u][1]) for u in group]
    for u in group:
        emit(u, lats[u][2], *ups[u])


def _mla_proj(stream, mod, params, rcs, rcst):
    b, _, d = stream[0].shape
    t = sum(a.shape[1] for a in stream)
    nt = t // TILE
    hds = MLA_HEADS
    g = _batch_group(b, MAX_BATCH_GROUP)
    q_scale = (MLA_NOPE + MLA_ROPE) ** -0.5 * LOG2E
    return pl.pallas_call(
        functools.partial(_mla_proj_kernel, n_stream=len(stream), q_scale=q_scale),
        out_shape=(
            jax.ShapeDtypeStruct((b, hds, MLA_QK, t), BF16),
            jax.ShapeDtypeStruct((b, hds, t, MLA_QK), BF16),
            jax.ShapeDtypeStruct((b, hds, MLA_VX, t), BF16),
        ),
        grid=(b // g, nt),
        in_specs=_stream_specs(stream, nt, d, g) + [_mod_spec(mod, g, nt, b)]
        + [_param_spec(p) for p in params] + [
            pl.BlockSpec((TILE, LANES), lambda bi, i: (i, 0)),
            pl.BlockSpec((LANES, TILE), lambda bi, i: (0, i)),
        ],
        out_specs=(
            pl.BlockSpec((g, hds, MLA_QK, TILE), lambda bi, i: (bi, 0, 0, i)),
            pl.BlockSpec((g, hds, TILE, MLA_QK), lambda bi, i: (bi, 0, i, 0)),
            pl.BlockSpec((g, hds, MLA_VX, TILE), lambda bi, i: (bi, 0, 0, i)),
        ),
        compiler_params=_params("mla_proj", ("parallel", "parallel")),
        name="mla_proj",
    )(*stream, mod.stack, *[p.stack for p in params], rcs, rcst)


def _attn_kernel(qt_ref, k_ref, vt_ref, o_ref, s_ref, p_ref, *, tq, tk, n_tiles, n_ctx):
    n_keys = k_ref.shape[2]
    n_chunks = n_keys // tk
    assert n_chunks >= 2

    def s_slot(j):
        return 2 if j == 0 else j % 2

    def q_start(i):
        return i * tq if isinstance(i, int) else pl.multiple_of(i * tq, tq)

    def scores(i, j):
        qt = qt_ref[0, 0, :, pl.ds(q_start(i), tq)]
        k = k_ref[0, 0, j * tk:(j + 1) * tk, :]
        s = _dot(k, qt)
        s_ref[s_slot(j)] = s
        return jnp.max(s, axis=0, keepdims=True)

    def softmax(j, m, s_max):
        m_new = jnp.maximum(m, s_max)
        p = jnp.exp2(s_ref[s_slot(j)] - m_new)
        p_ref[j % 2] = p.astype(BF16)
        return m_new, jnp.exp2(m - m_new)

    def pv(j, acc, alpha):
        vt = vt_ref[0, 0, :, j * tk:(j + 1) * tk]
        return alpha * acc + _dot(vt, p_ref[j % 2])

    def tile(i, s_max):
        m = jnp.full((1, tq), -jnp.inf, F32)
        acc = jnp.zeros((MLA_VX, tq), F32)
        alpha = None
        for j in range(n_chunks):
            if j > 0:
                acc = pv(j - 1, acc, alpha)
            if j + 1 < n_chunks:
                next_max = scores(i, j + 1)
            else:
                next_max = scores(jnp.minimum(i + 1, n_tiles - 1), 0)
            m, alpha = softmax(j, m, s_max)
            s_max = next_max
        acc = pv(n_chunks - 1, acc, alpha)
        o = acc[:MLA_V] * (1.0 / acc[MLA_V:MLA_V + 1])
        o_ref[0, pl.ds(q_start(i), tq), :] = o.T.astype(o_ref.dtype)
        return s_max

    lax.fori_loop(0, n_tiles, tile, scores(0, 0), unroll=math.gcd(n_tiles, 4))

    c0 = n_keys - n_ctx
    s = _dot(k_ref[0, 0, c0:, :], qt_ref[0, 0, :, c0:])
    p = jnp.exp2(s - jnp.max(s, axis=0, keepdims=True)).astype(BF16)
    acc = _dot(vt_ref[0, 0, :, c0:], p)
    o = acc[:MLA_V] * (1.0 / acc[MLA_V:MLA_V + 1])
    o_ref[0, c0:, :] = o.T.astype(o_ref.dtype)


def _attn_tk(n_keys):
    for tk in (768, 512, 256):
        if n_keys % tk == 0:
            return tk
    raise ValueError(f"unsupported key count {n_keys}")


def _mla_attention(qt, k, vt, n_lat, tq):
    b, hds, _, t = qt.shape
    tk = _attn_tk(t)
    return pl.pallas_call(
        functools.partial(_attn_kernel, tq=tq, tk=tk, n_tiles=n_lat // tq, n_ctx=t - n_lat),
        out_shape=jax.ShapeDtypeStruct((b, t, hds * MLA_V), BF16),
        grid=(b, hds),
        in_specs=[
            pl.BlockSpec((1, 1, MLA_QK, t), lambda bi, h: (bi, h, 0, 0)),
            pl.BlockSpec((1, 1, t, MLA_QK), lambda bi, h: (bi, h, 0, 0)),
            pl.BlockSpec((1, 1, MLA_VX, t), lambda bi, h: (bi, h, 0, 0)),
        ],
        out_specs=pl.BlockSpec((1, t, MLA_V), lambda bi, h: (bi, 0, h)),
        scratch_shapes=[pltpu.VMEM((3, tk, tq), F32), pltpu.VMEM((2, tk, tq), BF16)],
        compiler_params=_params("mla_attn", ("parallel", "parallel")),
        name="mla_attn",
    )(qt, k, vt)


def _gla_proj_kernel(x_ref, mod_ref, ng_ref, wg_ref, w2_ref, bg_ref,
                     qf_ref, kf_ref, kfe_ref, qb_ref, kb_ref, kbe_ref, v_ref, r_ref, d_ref):
    group = range(x_ref.shape[0])
    dk = GLA_HEADS * LANES
    dv = x_ref.shape[2]
    n_chunks = TILE // GLA_CHUNK

    def project(u):
        mod = mod_ref[u]
        h = (_rms(x_ref[u], ng_ref[0:1]) * (1.0 + mod[1:2]) + mod[0:1]).astype(BF16)
        return _dot(h, wg_ref[...])

    def gates(u, big):
        v_ref[u] = big[:, 2 * dk:2 * dk + dv].astype(BF16)
        rr = big[:, 2 * dk + dv:2 * dk + 2 * dv]
        r_ref[u] = (rr / (1.0 + jnp.exp(-rr))).astype(BF16)
        low = big[:, 2 * dk + 2 * dv:].astype(BF16)
        z = _dot(low, w2_ref[...]) + bg_ref[...]
        g = -(jnp.maximum(-z, 0.0) + jnp.log(1.0 + jnp.exp(-jnp.abs(z)))) * (1.0 / GLA_TAU)
        g_hi = g.astype(BF16)
        return g_hi, (g - g_hi.astype(F32)).astype(BF16)

    row = lax.broadcasted_iota(jnp.int32, (TILE, TILE), 0)
    col = lax.broadcasted_iota(jnp.int32, (TILE, TILE), 1)
    shift = GLA_CHUNK.bit_length() - 1
    same = (row >> shift) == (col >> shift)
    prefix = jnp.where(same & (col <= row), 1.0, 0.0).astype(BF16)
    suffix = jnp.where(same & (col >= row), 1.0, 0.0).astype(BF16)

    def cumsums(g_hi, g_lo):
        gf = _dot(prefix, g_hi[:, :dk]) + _dot(prefix, g_lo[:, :dk])
        gb = _dot(suffix, g_hi[:, dk:]) + _dot(suffix, g_lo[:, dk:])
        return gf, gb

    def emit(u, big, gf, gb):
        q = big[:, :dk] * (LANES ** -0.5)
        k = big[:, dk:2 * dk]
        tot_f = [gf[(c + 1) * GLA_CHUNK - 1:(c + 1) * GLA_CHUNK] for c in range(n_chunks)]
        tot_b = [gb[c * GLA_CHUNK:c * GLA_CHUNK + 1] for c in range(n_chunks)]
        dec_f = [jnp.exp(t) for t in tot_f]
        dec_b = [jnp.exp(t) for t in tot_b]
        for c in range(n_chunks):
            d_ref[u, 0, c:c + 1, 0:dk] = dec_f[c]
            d_ref[u, 0, c:c + 1, dk:2 * dk] = dec_b[c]
        d_ref[u, 0, n_chunks:, :] = jnp.ones((d_ref.shape[2] - n_chunks, 2 * dk), F32)
        kf = k * jnp.exp(-gf)
        kb = k * jnp.exp(-gb)
        qf_ref[u] = (q * jnp.exp(gf)).astype(BF16)
        kf_ref[u] = kf.astype(BF16)
        kfe_ref[u] = (kf * jnp.concatenate([jnp.broadcast_to(t, (GLA_CHUNK, dk)) for t in dec_f], axis=0)).astype(BF16)
        qb_ref[u] = (q * jnp.exp(gb)).astype(BF16)
        kb_ref[u] = kb.astype(BF16)
        kbe_ref[u] = (kb * jnp.concatenate([jnp.broadcast_to(t, (GLA_CHUNK, dk)) for t in dec_b], axis=0)).astype(BF16)

    pending = None
    for u in group:
        big = project(u)
        if pending is not None:
            emit(pending[0], pending[1], *cumsums(*pending[2]))
        pending = (u, big, gates(u, big))
    emit(pending[0], pending[1], *cumsums(*pending[2]))


def _gla_proj(xa, mod, params):
    b, t, d = xa.shape
    nt = t // TILE
    dk = GLA_HEADS * LANES
    g = _batch_group(b, MAX_BATCH_GROUP)
    row_spec = lambda w: pl.BlockSpec((g, TILE, w), lambda bi, i: (bi, i, 0))
    qk = jax.ShapeDtypeStruct((b, t, dk), BF16)
    wide = jax.ShapeDtypeStruct((b, t, d), BF16)
    return pl.pallas_call(
        _gla_proj_kernel,
        out_shape=(qk, qk, qk, qk, qk, qk, wide, wide,
                   jax.ShapeDtypeStruct((b, nt, GLA_DECAY_ROWS, 2 * dk), F32)),
        grid=(b // g, nt),
        in_specs=[row_spec(d), _mod_spec(mod, g, nt, b)] + [_param_spec(p) for p in params],
        out_specs=(row_spec(dk),) * 6 + (
            row_spec(d),
            row_spec(d),
            pl.BlockSpec((g, 1, GLA_DECAY_ROWS, 2 * dk), lambda bi, i: (bi, i, 0, 0)),
        ),
        compiler_params=_params("gla_proj", ("parallel", "parallel")),
        name="gla_proj",
    )(xa, mod.stack, *[p.stack for p in params])


def _gla_scan_kernel(qf_ref, kf_ref, kfe_ref, vf_ref, df_ref,
                     qb_ref, kb_ref, kbe_ref, vb_ref, db_ref,
                     of_ref, ob_ref, s_ref):
    @pl.when(pl.program_id(1) == 0)
    def _():
        s_ref[...] = jnp.zeros(s_ref.shape, s_ref.dtype)

    L = GLA_CHUNK
    n_chunks = TILE // L
    dk = LANES
    dv = vf_ref.shape[2] // GLA_HEADS
    shift = L.bit_length() - 1
    ri = lax.broadcasted_iota(jnp.int32, (TILE, TILE), 0)
    ci = lax.broadcasted_iota(jnp.int32, (TILE, TILE), 1)
    same_chunk = (ri >> shift) == (ci >> shift)
    causal = same_chunk & (ci <= ri)
    anti = same_chunk & (ci >= ri)
    token_chunk = lax.broadcasted_iota(jnp.int32, (dk, TILE), 1) >> shift

    directions = (
        (qf_ref, kf_ref, kfe_ref, vf_ref, df_ref, of_ref, causal, range(n_chunks)),
        (qb_ref, kb_ref, kbe_ref, vb_ref, db_ref, ob_ref, anti, range(n_chunks - 1, -1, -1)),
    )
    units = [(u, direction, hd) for u in range(qf_ref.shape[0]) for direction in range(2)
             for hd in range(GLA_HEADS)]

    def intra(u, direction, hd):
        q_ref, k_ref, _, _, _, _, mask, _ = directions[direction]
        kc = slice(hd * dk, (hd + 1) * dk)
        return jnp.where(mask, _dot_nt(q_ref[u, :, kc], k_ref[u, :, kc]), 0.0).astype(BF16)

    def values(u, direction, hd, a):
        _, _, ke_ref, v_ref, _, _, _, _ = directions[direction]
        kt = ke_ref[u, :, hd * dk:(hd + 1) * dk].astype(F32).T
        kt_blocks = [jnp.where(token_chunk == c, kt, 0.0).astype(BF16) for c in range(n_chunks)]
        return _dot(jnp.concatenate([a] + kt_blocks, axis=0), v_ref[u, :, hd * dv:(hd + 1) * dv])

    def recur(u, direction, hd, r):
        q_ref, _, _, _, d_ref, o_ref, _, order = directions[direction]
        col = direction * GLA_HEADS + hd
        idx = u * 2 * GLA_HEADS + col
        d_rows = d_ref[u, 0, :, col * dk:(col + 1) * dk]
        d_cols = jnp.concatenate([d_rows, jnp.zeros((dk - d_rows.shape[0], dk), F32)], axis=0).T
        st = s_ref[idx]
        for c in order:
            rows = slice(c * L, (c + 1) * L)
            o = r[rows] + _dot(q_ref[u, rows, hd * dk:(hd + 1) * dk], st.astype(BF16))
            o_ref[u, rows, hd * dv:(hd + 1) * dv] = o.astype(o_ref.dtype)
            st = st * d_cols[:, c:c + 1] + r[TILE + c * dk:TILE + (c + 1) * dk]
        s_ref[idx] = st

    a_all = [intra(*u) for u in units]
    r_all = [values(*u, a) for u, a in zip(units, a_all)]
    for u, r in zip(units, r_all):
        recur(*u, r)


def _gla_scan(qf, kf, kfe, qb, kb, kbe, v, dd):
    b, t, dkh = qf.shape
    d = v.shape[2]
    nt = t // TILE
    f_tile = lambda s: jnp.where(s == 0, nt - 1, s - 1)
    b_tile = lambda s: jnp.where(s == 0, nt - 1, nt - 1 - s)

    g = _batch_group(b)

    def specs(tile):
        row = lambda w: pl.BlockSpec((g, TILE, w), lambda bi, s: (bi, tile(s), 0))
        return [row(dkh), row(dkh), row(dkh), row(d),
                pl.BlockSpec((g, 1, GLA_DECAY_ROWS, 2 * dkh), lambda bi, s: (bi, tile(s), 0, 0))]

    out = jax.ShapeDtypeStruct((b, t, d), BF16)
    return pl.pallas_call(
        _gla_scan_kernel,
        out_shape=(out, out),
        grid=(b // g, nt),
        in_specs=specs(f_tile) + specs(b_tile),
        out_specs=(pl.BlockSpec((g, TILE, d), lambda bi, s: (bi, f_tile(s), 0)),
                   pl.BlockSpec((g, TILE, d), lambda bi, s: (bi, b_tile(s), 0))),
        scratch_shapes=[pltpu.VMEM((g * 2 * GLA_HEADS, LANES, d // GLA_HEADS), F32)],
        compiler_params=_params("gla_scan", ("parallel", "arbitrary")),
        name="gla_scan",
    )(qf, kf, kfe, v, dd, qb, kb, kbe, v, dd)


def _post_kernel(*refs, n_stream, gla):
    stream_refs = refs[:n_stream]
    refs = refs[n_stream:]
    if gla:
        mod_ref, ng_ref, of_ref, ob_ref, r_ref, go_ref, wo_ref, w1_ref, w2_ref, out_ref = refs
    else:
        mod_ref, ng_ref, o_ref, wo_ref, w1_ref, w2_ref, out_ref = refs
    ng = ng_ref[...]
    group = range(out_ref.shape[0])
    d = out_ref.shape[2]

    def mixer_proj(u):
        if gla:
            o = of_ref[u].astype(F32) + ob_ref[u].astype(F32)
            dv = d // GLA_HEADS
            go = go_ref[...]
            o = jnp.concatenate([_rms(o[:, i * dv:(i + 1) * dv], go) for i in range(GLA_HEADS)], axis=1)
            y_in = (o * r_ref[u].astype(F32)).astype(BF16)
        else:
            y_in = o_ref[u]
        return _dot(y_in, wo_ref[...])

    def residual(u, y):
        mod = mod_ref[u]
        x1 = _stream_tile(stream_refs, u) + mod[2:3] * _rms(y, ng[1:2])
        return x1, (_rms(x1, ng[2:3]) * (1.0 + mod[4:5]) + mod[3:4]).astype(BF16)

    def mlp(h):
        z = jnp.zeros((TILE, d), F32)
        for j in range(w1_ref.shape[1] // d):
            a = jnp.maximum(_dot(h, w1_ref[:, j * d:(j + 1) * d]), 0.0)
            z = z + _dot((a * a).astype(BF16), w2_ref[j * d:(j + 1) * d, :])
        return z

    ys = [mixer_proj(u) for u in group]
    zs, x1s = [], []
    for u in group:
        x1, h = residual(u, ys[u])
        x1s.append(x1)
        zs.append(mlp(h))
    for u in group:
        out_ref[u] = x1s[u] + mod_ref[u][5:6] * _rms(zs[u], ng[3:4])


def _post(stream, mod, ng, mixer_out, weights, gla, latent_only):
    b, _, d = stream[0].shape
    t = mixer_out[0].shape[1]
    nt = t // TILE
    if latent_only:
        stream = stream[:1]
    n_out = nt - 1 if latent_only else nt
    g = _batch_group(b)
    row = pl.BlockSpec((g, TILE, d), lambda bi, i: (bi, i, 0))
    in_place = len(stream) == 1 and not latent_only
    return pl.pallas_call(
        functools.partial(_post_kernel, n_stream=len(stream), gla=gla),
        out_shape=jax.ShapeDtypeStruct((b, n_out * TILE, d), stream[0].dtype),
        grid=(b // g, n_out),
        in_specs=_stream_specs(stream, nt, d, g)
        + [_mod_spec(mod, g, nt, b), _param_spec(ng)]
        + [row] * len(mixer_out) + [_param_spec(w) for w in weights],
        out_specs=row,
        input_output_aliases={0: 0} if in_place else {},
        compiler_params=_params("post", ("parallel", "parallel")),
        name="gla_post" if gla else "mla_post",
    )(*stream, mod.stack, ng.stack, *mixer_out, *[w.stack for w in weights])


def _rope_tables(n_lat, n_ctx):
    t = jnp.arange(n_lat)
    row = (t // GRID_W).astype(F32)
    col = (t % GRID_W).astype(F32)
    half = MLA_ROPE // 2
    inv_freq = ROPE_BASE ** (-jnp.arange(0, half, 2, dtype=F32) / half)
    ang_r = row[:, None] * inv_freq
    ang_c = col[:, None] * inv_freq
    cr, sr, cc, sc = jnp.cos(ang_r), jnp.sin(ang_r), jnp.cos(ang_c), jnp.sin(ang_c)
    cos = jnp.concatenate([cr, cr, cc, cc], axis=1)
    sin = jnp.concatenate([-sr, sr, -sc, sc], axis=1)
    cos = jnp.concatenate([cos, jnp.ones((n_ctx, MLA_ROPE), F32)], axis=0)
    sin = jnp.concatenate([sin, jnp.zeros((n_ctx, MLA_ROPE), F32)], axis=0)
    assert 2 * MLA_ROPE == LANES
    return jnp.concatenate([cos, sin], axis=1)


def _swap_perm():
    q = MLA_ROPE // 4
    return jnp.array(list(range(q, 2 * q)) + list(range(0, q)) + list(range(3 * q, 4 * q)) + list(range(2 * q, 3 * q)))


def _mla_weights(w_dq, g_q, w_uq, w_dkv, g_kv, w_ukv, w_o):
    n = w_dq.shape[0]
    perm = _swap_perm()
    rope = w_dkv[:, :, MLA_KV_RANK:]
    wd = jnp.concatenate([w_dq, w_dkv[:, :, :MLA_KV_RANK], rope, rope[:, :, perm]], axis=2)
    uq = w_uq.reshape(n, MLA_Q_RANK, MLA_HEADS, MLA_NOPE + MLA_ROPE)
    q_nope = uq[..., :MLA_NOPE].reshape(n, MLA_Q_RANK, -1)
    q_rope = uq[..., MLA_NOPE:]
    q_swap = q_rope[..., perm]
    wq = jnp.concatenate([q_nope, q_rope.reshape(n, MLA_Q_RANK, -1), q_swap.reshape(n, MLA_Q_RANK, -1)], axis=2)
    ukv = w_ukv.reshape(n, MLA_KV_RANK, MLA_HEADS, MLA_NOPE + MLA_V)
    wk = ukv[..., :MLA_NOPE].reshape(n, MLA_KV_RANK, -1)
    wv = ukv[..., MLA_NOPE:].reshape(n, MLA_KV_RANK, -1)
    proj = (wd.astype(BF16), g_q[:, None, :], g_kv[:, None, :], wq.astype(BF16).transpose(0, 2, 1),
            wk.astype(BF16), wv.astype(BF16).transpose(0, 2, 1))
    return proj, w_o.astype(BF16)


def _gla_weights(w_q, w_k, w_v, w_r, w_gate1, w_gate2, b_gate, g_o, w_o):
    n, d, dk = w_q.shape
    low_pad = jnp.zeros((n, d, LANES - 2 * GLA_RANK), w_q.dtype)
    wg = jnp.concatenate([w_q, w_k, w_v, w_r, w_gate1[:, 0], w_gate1[:, 1], low_pad], axis=2)
    zero = jnp.zeros((n, GLA_RANK, dk), w_gate2.dtype)
    w2 = jnp.concatenate([
        jnp.concatenate([w_gate2[:, 0], zero], axis=2),
        jnp.concatenate([zero, w_gate2[:, 1]], axis=2),
        jnp.zeros((n, LANES - 2 * GLA_RANK, 2 * dk), w_gate2.dtype)], axis=1)
    bg = jnp.concatenate([b_gate[:, 0], b_gate[:, 1]], axis=1)[:, None, :]
    return (wg.astype(BF16), w2.astype(BF16), bg), g_o[:, None, :], w_o.astype(BF16)


def kernel(x, c, ctx, c_ctx, ada_w, ada_b, norm_g, mlp_w1, mlp_w2, mla_w_dq, mla_g_q, mla_w_uq, mla_w_dkv, mla_g_kv, mla_w_ukv, mla_w_o, gla_w_q, gla_w_k, gla_w_v, gla_w_r, gla_w_gate1, gla_w_gate2, gla_b_gate, gla_g_o, gla_w_o):
    b, s, d = x.shape
    n_ctx = ctx.shape[1]
    depth = ada_w.shape[0]
    assert n_ctx == TILE and s % TILE == 0 and d == GLA_HEADS * 2 * LANES

    stream = (x, ctx)
    g = _batch_group(b, MAX_BATCH_GROUP)
    rows = -(-(b + g) // 16) * 16
    cvec = jnp.concatenate([c, jnp.tile(c_ctx[None], (g, 1)), jnp.zeros((rows - b - g, d), c.dtype)], axis=0)
    mod_all = _ada(cvec, ada_w, ada_b).reshape(depth, rows, N_MOD, d)

    rcs = _rope_tables(s, n_ctx)
    rcst = rcs.T
    tq = 512 if s % 512 == 0 else TILE

    w1_all = mlp_w1.astype(BF16)
    w2_all = mlp_w2.astype(BF16)
    mla_proj_w, mla_wo = _mla_weights(mla_w_dq, mla_g_q, mla_w_uq, mla_w_dkv, mla_g_kv, mla_w_ukv, mla_w_o)
    gla_proj_w, gla_go, gla_wo = _gla_weights(gla_w_q, gla_w_k, gla_w_v, gla_w_r, gla_w_gate1, gla_w_gate2,
                                              gla_b_gate, gla_g_o, gla_w_o)

    for i in range(depth):
        j = i // 2
        mod = _LayerParam(mod_all, i)
        ng = _LayerParam(norm_g, i)
        mlp = (_LayerParam(w1_all, i), _LayerParam(w2_all, i))
        last = i == depth - 1
        if i % 2 == 0:
            qt, k, vt = _mla_proj(stream, mod, [ng] + [_LayerParam(w, j) for w in mla_proj_w], rcs, rcst)
            o = _mla_attention(qt, k, vt, s, tq)
            weights = (_LayerParam(mla_wo, j),) + mlp
            stream = (_post(stream, mod, ng, (o,), weights, gla=False, latent_only=last),)
        else:
            (xa,) = stream
            qf, kf, kfe, qb, kb, kbe, v, r, dd = _gla_proj(xa, mod, [ng] + [_LayerParam(w, j) for w in gla_proj_w])
            of, ob = _gla_scan(qf, kf, kfe, qb, kb, kbe, v, dd)
            weights = (_LayerParam(gla_go, j), _LayerParam(gla_wo, j)) + mlp
            stream = (_post(stream, mod, ng, (of, ob, r), weights, gla=True, latent_only=last),)
    return stream[0]
```

```python
import functools
import math
from typing import NamedTuple

import jax
import jax.numpy as jnp
from jax import lax
from jax.experimental import pallas as pl
from jax.experimental.pallas import tpu as pltpu

N_MOD = 6
NORM_EPS = 1e-6
GRID_W = 64
MLA_HEADS = 8
MLA_NOPE = 128
MLA_ROPE = 64
MLA_V = 128
MLA_VX = MLA_V + 16
MLA_Q_RANK = 384
MLA_KV_RANK = 256
ROPE_BASE = 10000.0
GLA_HEADS = 4
GLA_RANK = 16
GLA_TAU = 16.0
GLA_CHUNK = 64

LANES = 128
TILE = 256
MLA_QK = 2 * LANES
GLA_DECAY_ROWS = 8
ADA_COLS = 1536
MIB = 1024 * 1024
LOG2E = 1.4426950408889634

F32 = jnp.float32
BF16 = jnp.bfloat16
NT = (((1,), (1,)), ((), ()))


def _dot(a, b):
    return jnp.dot(a, b, preferred_element_type=F32)


def _dot_nt(a, b):
    return lax.dot_general(a, b, NT, preferred_element_type=F32)


def _rms(x, g):
    ms = jnp.mean(x * x, axis=-1, keepdims=True)
    return x * lax.rsqrt(ms + NORM_EPS) * g


class _LayerParam(NamedTuple):
    stack: jax.Array
    layer: int


def _param_spec(p):
    zeros = (0,) * (p.stack.ndim - 1)
    return pl.BlockSpec((None,) + p.stack.shape[1:], lambda *_: (p.layer,) + zeros,
                        pipeline_mode=pl.Buffered(1))


def _mod_spec(mod, group, n_tiles, n_batch):
    d = mod.stack.shape[-1]
    return pl.BlockSpec((None, group, N_MOD, d),
                        lambda bi, i: (mod.layer, jnp.where(i == n_tiles - 1, n_batch // group, bi), 0, 0))


VMEM_MIB = {
    "ada": 40,
    "mla_proj": 48,
    "mla_attn": 56,
    "gla_proj": 56,
    "gla_scan": 48,
    "post": 56,
}


def _params(name, sem):
    return pltpu.CompilerParams(dimension_semantics=sem, vmem_limit_bytes=VMEM_MIB[name] * MIB)


def _stream_specs(stream, n_tiles, d, nb=1):
    if len(stream) == 1:
        return [pl.BlockSpec((nb, TILE, d), lambda bi, i: (bi, i, 0))]
    last_lat = n_tiles - 2
    return [pl.BlockSpec((nb, TILE, d), lambda bi, i: (bi, jnp.minimum(i, last_lat), 0)),
            pl.BlockSpec((nb, TILE, d), lambda bi, i: (bi, 0, 0))]


def _stream_tile(stream_refs, u=0):
    if len(stream_refs) == 1:
        return stream_refs[0][u]
    is_ctx = pl.program_id(1) == pl.num_programs(1) - 1
    return jnp.where(is_ctx, stream_refs[1][u], stream_refs[0][u])


MAX_BATCH_GROUP = 4


def _batch_group(b, limit=2):
    g = limit
    while b % g:
        g //= 2
    return g


def _ada_kernel(c_ref, w_ref, b_ref, o_ref):
    c = c_ref[...]
    a = c / (1.0 + jnp.exp(-c))
    a_hi = a.astype(BF16)
    a_lo = (a - a_hi.astype(F32)).astype(BF16)
    w = w_ref[0]
    w_hi = w.astype(BF16)
    w_lo = (w - w_hi.astype(F32)).astype(BF16)
    o_ref[0] = _dot(a_hi, w_hi) + _dot(a_lo, w_hi) + _dot(a_hi, w_lo) + b_ref[0]


def _ada(cvec, ada_w, ada_b):
    depth, d, n = ada_w.shape
    rows = cvec.shape[0]
    tn = ADA_COLS
    assert n % tn == 0
    return pl.pallas_call(
        _ada_kernel,
        out_shape=jax.ShapeDtypeStruct((depth, rows, n), F32),
        grid=(depth, n // tn),
        in_specs=[
            pl.BlockSpec((rows, d), lambda l, j: (0, 0)),
            pl.BlockSpec((1, d, tn), lambda l, j: (l, 0, j)),
            pl.BlockSpec((1, 1, tn), lambda l, j: (l, 0, j)),
        ],
        out_specs=pl.BlockSpec((1, rows, tn), lambda l, j: (l, 0, j)),
        compiler_params=_params("ada", ("parallel", "parallel")),
        name="ada",
    )(cvec, ada_w, ada_b.reshape(depth, 1, n))


def _mla_proj_kernel(*refs, n_stream, q_scale):
    (mod_ref, ng_ref, wd_ref, gq_ref, gkv_ref, wqt_ref, wk_ref, wvt_ref,
     rcs_ref, rcst_ref, qt_ref, k_ref, vt_ref) = refs[n_stream:]
    stream_refs = refs[:n_stream]
    group = range(qt_ref.shape[0])
    nq, nkv, nr = MLA_Q_RANK, MLA_KV_RANK, MLA_ROPE
    hw = MLA_HEADS * LANES
    hr = MLA_HEADS * nr

    def down(u):
        mod = mod_ref[u]
        h = (_rms(_stream_tile(stream_refs, u), ng_ref[0:1]) * (1.0 + mod[1:2]) + mod[0:1]).astype(BF16)
        return _dot(h, wd_ref[...])

    def latents(a):
        cq = _rms(a[:, :nq], gq_ref[...]).astype(BF16)
        ckv = _rms(a[:, nq:nq + nkv], gkv_ref[...]).astype(BF16)
        y = a[:, nq + nkv:] * rcs_ref[...]
        lane = lax.broadcasted_iota(jnp.int32, y.shape, 1)
        k_rope = jnp.where(lane < nr, y + pltpu.roll(y, nr, axis=1), 0.0).astype(BF16)
        return cq, ckv, k_rope

    def up(cq, ckv):
        qt = _dot_nt(wqt_ref[...], cq) * q_scale
        k_nope = _dot(ckv, wk_ref[...]).astype(BF16)
        vt = _dot_nt(wvt_ref[...], ckv).astype(BF16)
        return qt, k_nope, vt

    def emit(u, k_rope, qt, k_nope, vt):
        cos_t = rcst_ref[0:nr]
        sin_t = rcst_ref[nr:2 * nr]
        for hd in range(MLA_HEADS):
            s = hd * LANES
            r = hw + hd * nr
            qt_ref[u, hd, 0:LANES, :] = qt[s:s + LANES].astype(BF16)
            q_rope = qt[r:r + nr] * cos_t + qt[r + hr:r + hr + nr] * sin_t
            qt_ref[u, hd, LANES:LANES + nr, :] = q_rope.astype(BF16)
            qt_ref[u, hd, LANES + nr:, :] = jnp.zeros((MLA_QK - LANES - nr, TILE), BF16)
            k_ref[u, hd, :, 0:LANES] = k_nope[:, s:s + LANES]
            k_ref[u, hd, :, LANES:2 * LANES] = k_rope
            vt_ref[u, hd, 0:MLA_V] = vt[s:s + LANES]
            vt_ref[u, hd, MLA_V:MLA_VX] = jnp.ones((MLA_VX - MLA_V, TILE), BF16)

    downs = [down(u) for u in group]
    lats = [latents(downs[u]) for u in group]
    ups = [up(lats[u][0], lats[u][1]) for u in group]
    for u in group:
        emit(u, lats[u][2], *ups[u])


def _mla_proj(stream, mod, params, rcs, rcst):
    b, _, d = stream[0].shape
    t = sum(a.shape[1] for a in stream)
    nt = t // TILE
    hds = MLA_HEADS
    g = _batch_group(b, MAX_BATCH_GROUP)
    q_scale = (MLA_NOPE + MLA_ROPE) ** -0.5 * LOG2E
    return pl.pallas_call(
        functools.partial(_mla_proj_kernel, n_stream=len(stream), q_scale=q_scale),
        out_shape=(
            jax.ShapeDtypeStruct((b, hds, MLA_QK, t), BF16),
            jax.ShapeDtypeStruct((b, hds, t, MLA_QK), BF16),
            jax.ShapeDtypeStruct((b, hds, MLA_VX, t), BF16),
        ),
        grid=(b // g, nt),
        in_specs=_stream_specs(stream, nt, d, g) + [_mod_spec(mod, g, nt, b)]
        + [_param_spec(p) for p in params] + [
            pl.BlockSpec((TILE, LANES), lambda bi, i: (i, 0)),
            pl.BlockSpec((LANES, TILE), lambda bi, i: (0, i)),
        ],
        out_specs=(
            pl.BlockSpec((g, hds, MLA_QK, TILE), lambda bi, i: (bi, 0, 0, i)),
            pl.BlockSpec((g, hds, TILE, MLA_QK), lambda bi, i: (bi, 0, i, 0)),
            pl.BlockSpec((g, hds, MLA_VX, TILE), lambda bi, i: (bi, 0, 0, i)),
        ),
        compiler_params=_params("mla_proj", ("parallel", "parallel")),
        name="mla_proj",
    )(*stream, mod.stack, *[p.stack for p in params], rcs, rcst)


def _attn_kernel(qt_ref, k_ref, vt_ref, o_ref, s_ref, p_ref, *, tq, tk, n_tiles, n_ctx):
    n_keys = k_ref.shape[2]
    n_chunks = n_keys // tk
    assert n_chunks >= 2

    def s_slot(j):
        return 2 if j == 0 else j % 2

    def q_start(i):
        return i * tq if isinstance(i, int) else pl.multiple_of(i * tq, tq)

    def scores(i, j):
        qt = qt_ref[0, 0, :, pl.ds(q_start(i), tq)]
        k = k_ref[0, 0, j * tk:(j + 1) * tk, :]
        s = _dot(k, qt)
        s_ref[s_slot(j)] = s
        return jnp.max(s, axis=0, keepdims=True)

    def softmax(j, m, s_max):
        m_new = jnp.maximum(m, s_max)
        p = jnp.exp2(s_ref[s_slot(j)] - m_new)
        p_ref[j % 2] = p.astype(BF16)
        return m_new, jnp.exp2(m - m_new)

    def pv(j, acc, alpha):
        vt = vt_ref[0, 0, :, j * tk:(j + 1) * tk]
        return alpha * acc + _dot(vt, p_ref[j % 2])

    def tile(i, s_max):
        m = jnp.full((1, tq), -jnp.inf, F32)
        acc = jnp.zeros((MLA_VX, tq), F32)
        alpha = None
        for j in range(n_chunks):
            if j > 0:
                acc = pv(j - 1, acc, alpha)
            if j + 1 < n_chunks:
                next_max = scores(i, j + 1)
            else:
                next_max = scores(jnp.minimum(i + 1, n_tiles - 1), 0)
            m, alpha = softmax(j, m, s_max)
            s_max = next_max
        acc = pv(n_chunks - 1, acc, alpha)
        o = acc[:MLA_V] * (1.0 / acc[MLA_V:MLA_V + 1])
        o_ref[0, pl.ds(q_start(i), tq), :] = o.T.astype(o_ref.dtype)
        return s_max

    c0 = n_keys - n_ctx
    s = _dot(k_ref[0, 0, c0:, :], qt_ref[0, 0, :, c0:])
    first_max = scores(0, 0)
    p = jnp.exp2(s - jnp.max(s, axis=0, keepdims=True)).astype(BF16)
    acc = _dot(vt_ref[0, 0, :, c0:], p)
    o = acc[:MLA_V] * (1.0 / acc[MLA_V:MLA_V + 1])
    o_ref[0, c0:, :] = o.T.astype(o_ref.dtype)

    lax.fori_loop(0, n_tiles, tile, first_max, unroll=math.gcd(n_tiles, 4))


def _attn_tk(n_keys):
    for tk in (768, 512, 256):
        if n_keys % tk == 0:
            return tk
    raise ValueError(f"unsupported key count {n_keys}")


def _mla_attention(qt, k, vt, n_lat, tq):
    b, hds, _, t = qt.shape
    tk = _attn_tk(t)
    return pl.pallas_call(
        functools.partial(_attn_kernel, tq=tq, tk=tk, n_tiles=n_lat // tq, n_ctx=t - n_lat),
        out_shape=jax.ShapeDtypeStruct((b, t, hds * MLA_V), BF16),
        grid=(b, hds),
        in_specs=[
            pl.BlockSpec((1, 1, MLA_QK, t), lambda bi, h: (bi, h, 0, 0)),
            pl.BlockSpec((1, 1, t, MLA_QK), lambda bi, h: (bi, h, 0, 0)),
            pl.BlockSpec((1, 1, MLA_VX, t), lambda bi, h: (bi, h, 0, 0)),
        ],
        out_specs=pl.BlockSpec((1, t, MLA_V), lambda bi, h: (bi, 0, h)),
        scratch_shapes=[pltpu.VMEM((3, tk, tq), F32), pltpu.VMEM((2, tk, tq), BF16)],
        compiler_params=_params("mla_attn", ("parallel", "parallel")),
        name="mla_attn",
    )(qt, k, vt)


def _gla_proj_kernel(x_ref, mod_ref, ng_ref, wg_ref, w2_ref, bg_ref,
                     qf_ref, kf_ref, kfe_ref, qb_ref, kb_ref, kbe_ref, v_ref, r_ref, d_ref):
    group = range(x_ref.shape[0])
    dk = GLA_HEADS * LANES
    dv = x_ref.shape[2]
    n_chunks = TILE // GLA_CHUNK

    def project(u):
        mod = mod_ref[u]
        h = (_rms(x_ref[u], ng_ref[0:1]) * (1.0 + mod[1:2]) + mod[0:1]).astype(BF16)
        return _dot(h, wg_ref[...])

    def gates(u, big):
        v_ref[u] = big[:, 2 * dk:2 * dk + dv].astype(BF16)
        rr = big[:, 2 * dk + dv:2 * dk + 2 * dv]
        r_ref[u] = (rr / (1.0 + jnp.exp(-rr))).astype(BF16)
        low = big[:, 2 * dk + 2 * dv:].astype(BF16)
        z = _dot(low, w2_ref[...]) + bg_ref[...]
        g = -(jnp.maximum(-z, 0.0) + jnp.log(1.0 + jnp.exp(-jnp.abs(z)))) * (1.0 / GLA_TAU)
        g_hi = g.astype(BF16)
        return g_hi, (g - g_hi.astype(F32)).astype(BF16)

    row = lax.broadcasted_iota(jnp.int32, (TILE, TILE), 0)
    col = lax.broadcasted_iota(jnp.int32, (TILE, TILE), 1)
    shift = GLA_CHUNK.bit_length() - 1
    same = (row >> shift) == (col >> shift)
    prefix = jnp.where(same & (col <= row), 1.0, 0.0).astype(BF16)
    suffix = jnp.where(same & (col >= row), 1.0, 0.0).astype(BF16)

    def cumsums(g_hi, g_lo):
        gf = _dot(prefix, g_hi[:, :dk]) + _dot(prefix, g_lo[:, :dk])
        gb = _dot(suffix, g_hi[:, dk:]) + _dot(suffix, g_lo[:, dk:])
        return gf, gb

    def emit(u, big, gf, gb):
        q = big[:, :dk] * (LANES ** -0.5)
        k = big[:, dk:2 * dk]
        tot_f = [gf[(c + 1) * GLA_CHUNK - 1:(c + 1) * GLA_CHUNK] for c in range(n_chunks)]
        tot_b = [gb[c * GLA_CHUNK:c * GLA_CHUNK + 1] for c in range(n_chunks)]
        dec_f = [jnp.exp(t) for t in tot_f]
        dec_b = [jnp.exp(t) for t in tot_b]
        for c in range(n_chunks):
            d_ref[u, 0, c:c + 1, 0:dk] = dec_f[c]
            d_ref[u, 0, c:c + 1, dk:2 * dk] = dec_b[c]
        d_ref[u, 0, n_chunks:, :] = jnp.ones((d_ref.shape[2] - n_chunks, 2 * dk), F32)
        kf = k * jnp.exp(-gf)
        kb = k * jnp.exp(-gb)
        qf_ref[u] = (q * jnp.exp(gf)).astype(BF16)
        kf_ref[u] = kf.astype(BF16)
        kfe_ref[u] = (kf * jnp.concatenate([jnp.broadcast_to(t, (GLA_CHUNK, dk)) for t in dec_f], axis=0)).astype(BF16)
        qb_ref[u] = (q * jnp.exp(gb)).astype(BF16)
        kb_ref[u] = kb.astype(BF16)
        kbe_ref[u] = (kb * jnp.concatenate([jnp.broadcast_to(t, (GLA_CHUNK, dk)) for t in dec_b], axis=0)).astype(BF16)

    pending = None
    for u in group:
        big = project(u)
        if pending is not None:
            emit(pending[0], pending[1], *cumsums(*pending[2]))
        pending = (u, big, gates(u, big))
    emit(pending[0], pending[1], *cumsums(*pending[2]))


def _gla_proj(xa, mod, params):
    b, t, d = xa.shape
    nt = t // TILE
    dk = GLA_HEADS * LANES
    g = _batch_group(b, MAX_BATCH_GROUP)
    row_spec = lambda w: pl.BlockSpec((g, TILE, w), lambda bi, i: (bi, i, 0))
    qk = jax.ShapeDtypeStruct((b, t, dk), BF16)
    wide = jax.ShapeDtypeStruct((b, t, d), BF16)
    return pl.pallas_call(
        _gla_proj_kernel,
        out_shape=(qk, qk, qk, qk, qk, qk, wide, wide,
                   jax.ShapeDtypeStruct((b, nt, GLA_DECAY_ROWS, 2 * dk), F32)),
        grid=(b // g, nt),
        in_specs=[row_spec(d), _mod_spec(mod, g, nt, b)] + [_param_spec(p) for p in params],
        out_specs=(row_spec(dk),) * 6 + (
            row_spec(d),
            row_spec(d),
            pl.BlockSpec((g, 1, GLA_DECAY_ROWS, 2 * dk), lambda bi, i: (bi, i, 0, 0)),
        ),
        compiler_params=_params("gla_proj", ("parallel", "parallel")),
        name="gla_proj",
    )(xa, mod.stack, *[p.stack for p in params])


def _gla_scan_kernel(qf_ref, kf_ref, kfe_ref, vf_ref, df_ref,
                     qb_ref, kb_ref, kbe_ref, vb_ref, db_ref,
                     of_ref, ob_ref, s_ref):
    @pl.when(pl.program_id(1) == 0)
    def _():
        s_ref[...] = jnp.zeros(s_ref.shape, s_ref.dtype)

    L = GLA_CHUNK
    n_chunks = TILE // L
    dk = LANES
    dv = vf_ref.shape[2] // GLA_HEADS
    shift = L.bit_length() - 1
    ri = lax.broadcasted_iota(jnp.int32, (TILE, TILE), 0)
    ci = lax.broadcasted_iota(jnp.int32, (TILE, TILE), 1)
    same_chunk = (ri >> shift) == (ci >> shift)
    causal = same_chunk & (ci <= ri)
    anti = same_chunk & (ci >= ri)
    token_chunk = lax.broadcasted_iota(jnp.int32, (dk, TILE), 1) >> shift

    directions = (
        (qf_ref, kf_ref, kfe_ref, vf_ref, df_ref, of_ref, causal, range(n_chunks)),
        (qb_ref, kb_ref, kbe_ref, vb_ref, db_ref, ob_ref, anti, range(n_chunks - 1, -1, -1)),
    )
    units = [(u, direction, hd) for u in range(qf_ref.shape[0]) for direction in range(2)
             for hd in range(GLA_HEADS)]

    def intra(u, direction, hd):
        q_ref, k_ref, _, _, _, _, mask, _ = directions[direction]
        kc = slice(hd * dk, (hd + 1) * dk)
        return jnp.where(mask, _dot_nt(q_ref[u, :, kc], k_ref[u, :, kc]), 0.0).astype(BF16)

    def values(u, direction, hd, a):
        _, _, ke_ref, v_ref, _, _, _, _ = directions[direction]
        kt = ke_ref[u, :, hd * dk:(hd + 1) * dk].astype(F32).T
        kt_blocks = [jnp.where(token_chunk == c, kt, 0.0).astype(BF16) for c in range(n_chunks)]
        return _dot(jnp.concatenate([a] + kt_blocks, axis=0), v_ref[u, :, hd * dv:(hd + 1) * dv])

    def recur(u, direction, hd, r):
        q_ref, _, _, _, d_ref, o_ref, _, order = directions[direction]
        col = direction * GLA_HEADS + hd
        idx = u * 2 * GLA_HEADS + col
        d_rows = d_ref[u, 0, :, col * dk:(col + 1) * dk]
        d_cols = jnp.concatenate([d_rows, jnp.zeros((dk - d_rows.shape[0], dk), F32)], axis=0).T
        st = s_ref[idx]
        for c in order:
            rows = slice(c * L, (c + 1) * L)
            o = r[rows] + _dot(q_ref[u, rows, hd * dk:(hd + 1) * dk], st.astype(BF16))
            o_ref[u, rows, hd * dv:(hd + 1) * dv] = o.astype(o_ref.dtype)
            st = st * d_cols[:, c:c + 1] + r[TILE + c * dk:TILE + (c + 1) * dk]
        s_ref[idx] = st

    a_all = [intra(*u) for u in units]
    r_all = [values(*u, a) for u, a in zip(units, a_all)]
    for u, r in zip(units, r_all):
        recur(*u, r)


def _gla_scan(qf, kf, kfe, qb, kb, kbe, v, dd):
    b, t, dkh = qf.shape
    d = v.shape[2]
    nt = t // TILE
    f_tile = lambda s: jnp.where(s == 0, nt - 1, s - 1)
    b_tile = lambda s: jnp.where(s == 0, nt - 1, nt - 1 - s)

    g = _batch_group(b)

    def specs(tile):
        row = lambda w: pl.BlockSpec((g, TILE, w), lambda bi, s: (bi, tile(s), 0))
        return [row(dkh), row(dkh), row(dkh), row(d),
                pl.BlockSpec((g, 1, GLA_DECAY_ROWS, 2 * dkh), lambda bi, s: (bi, tile(s), 0, 0))]

    out = jax.ShapeDtypeStruct((b, t, d), BF16)
    return pl.pallas_call(
        _gla_scan_kernel,
        out_shape=(out, out),
        grid=(b // g, nt),
        in_specs=specs(f_tile) + specs(b_tile),
        out_specs=(pl.BlockSpec((g, TILE, d), lambda bi, s: (bi, f_tile(s), 0)),
                   pl.BlockSpec((g, TILE, d), lambda bi, s: (bi, b_tile(s), 0))),
        scratch_shapes=[pltpu.VMEM((g * 2 * GLA_HEADS, LANES, d // GLA_HEADS), F32)],
        compiler_params=_params("gla_scan", ("parallel", "arbitrary")),
        name="gla_scan",
    )(qf, kf, kfe, v, dd, qb, kb, kbe, v, dd)


def _post_kernel(*refs, n_stream, gla):
    stream_refs = refs[:n_stream]
    refs = refs[n_stream:]
    if gla:
        mod_ref, ng_ref, of_ref, ob_ref, r_ref, go_ref, wo_ref, w1_ref, w2_ref, out_ref = refs
    else:
        mod_ref, ng_ref, o_ref, wo_ref, w1_ref, w2_ref, out_ref = refs
    ng = ng_ref[...]
    group = range(out_ref.shape[0])
    d = out_ref.shape[2]

    def mixer_proj(u):
        if gla:
            o = of_ref[u].astype(F32) + ob_ref[u].astype(F32)
            dv = d // GLA_HEADS
            go = go_ref[...]
            o = jnp.concatenate([_rms(o[:, i * dv:(i + 1) * dv], go) for i in range(GLA_HEADS)], axis=1)
            y_in = (o * r_ref[u].astype(F32)).astype(BF16)
        else:
            y_in = o_ref[u]
        return _dot(y_in, wo_ref[...])

    def residual(u, y):
        mod = mod_ref[u]
        x1 = _stream_tile(stream_refs, u) + mod[2:3] * _rms(y, ng[1:2])
        return x1, (_rms(x1, ng[2:3]) * (1.0 + mod[4:5]) + mod[3:4]).astype(BF16)

    def mlp(h):
        a = jnp.maximum(_dot(h, w1_ref[...]), 0.0)
        return _dot((a * a).astype(BF16), w2_ref[...])

    ys = [mixer_proj(u) for u in group]
    zs, x1s = [], []
    for u in group:
        x1, h = residual(u, ys[u])
        x1s.append(x1)
        zs.append(mlp(h))
    for u in group:
        out_ref[u] = x1s[u] + mod_ref[u][5:6] * _rms(zs[u], ng[3:4])


def _post(stream, mod, ng, mixer_out, weights, gla, latent_only):
    b, _, d = stream[0].shape
    t = mixer_out[0].shape[1]
    nt = t // TILE
    if latent_only:
        stream = stream[:1]
    n_out = nt - 1 if latent_only else nt
    g = _batch_group(b)
    row = pl.BlockSpec((g, TILE, d), lambda bi, i: (bi, i, 0))
    in_place = len(stream) == 1 and not latent_only
    return pl.pallas_call(
        functools.partial(_post_kernel, n_stream=len(stream), gla=gla),
        out_shape=jax.ShapeDtypeStruct((b, n_out * TILE, d), stream[0].dtype),
        grid=(b // g, n_out),
        in_specs=_stream_specs(stream, nt, d, g)
        + [_mod_spec(mod, g, nt, b), _param_spec(ng)]
        + [row] * len(mixer_out) + [_param_spec(w) for w in weights],
        out_specs=row,
        input_output_aliases={0: 0} if in_place else {},
        compiler_params=_params("post", ("parallel", "parallel")),
        name="gla_post" if gla else "mla_post",
    )(*stream, mod.stack, ng.stack, *mixer_out, *[w.stack for w in weights])


def _rope_tables(n_lat, n_ctx):
    t = jnp.arange(n_lat)
    row = (t // GRID_W).astype(F32)
    col = (t % GRID_W).astype(F32)
    half = MLA_ROPE // 2
    inv_freq = ROPE_BASE ** (-jnp.arange(0, half, 2, dtype=F32) / half)
    ang_r = row[:, None] * inv_freq
    ang_c = col[:, None] * inv_freq
    cr, sr, cc, sc = jnp.cos(ang_r), jnp.sin(ang_r), jnp.cos(ang_c), jnp.sin(ang_c)
    cos = jnp.concatenate([cr, cr, cc, cc], axis=1)
    sin = jnp.concatenate([-sr, sr, -sc, sc], axis=1)
    cos = jnp.concatenate([cos, jnp.ones((n_ctx, MLA_ROPE), F32)], axis=0)
    sin = jnp.concatenate([sin, jnp.zeros((n_ctx, MLA_ROPE), F32)], axis=0)
    assert 2 * MLA_ROPE == LANES
    return jnp.concatenate([cos, sin], axis=1)


def _swap_perm():
    q = MLA_ROPE // 4
    return jnp.array(list(range(q, 2 * q)) + list(range(0, q)) + list(range(3 * q, 4 * q)) + list(range(2 * q, 3 * q)))


def _mla_weights(w_dq, g_q, w_uq, w_dkv, g_kv, w_ukv, w_o):
    n = w_dq.shape[0]
    perm = _swap_perm()
    rope = w_dkv[:, :, MLA_KV_RANK:]
    wd = jnp.concatenate([w_dq, w_dkv[:, :, :MLA_KV_RANK], rope, rope[:, :, perm]], axis=2)
    uq = w_uq.reshape(n, MLA_Q_RANK, MLA_HEADS, MLA_NOPE + MLA_ROPE)
    q_nope = uq[..., :MLA_NOPE].reshape(n, MLA_Q_RANK, -1)
    q_rope = uq[..., MLA_NOPE:]
    q_swap = q_rope[..., perm]
    wq = jnp.concatenate([q_nope, q_rope.reshape(n, MLA_Q_RANK, -1), q_swap.reshape(n, MLA_Q_RANK, -1)], axis=2)
    ukv = w_ukv.reshape(n, MLA_KV_RANK, MLA_HEADS, MLA_NOPE + MLA_V)
    wk = ukv[..., :MLA_NOPE].reshape(n, MLA_KV_RANK, -1)
    wv = ukv[..., MLA_NOPE:].reshape(n, MLA_KV_RANK, -1)
    proj = (wd.astype(BF16), g_q[:, None, :], g_kv[:, None, :], wq.astype(BF16).transpose(0, 2, 1),
            wk.astype(BF16), wv.astype(BF16).transpose(0, 2, 1))
    return proj, w_o.astype(BF16)


def _gla_weights(w_q, w_k, w_v, w_r, w_gate1, w_gate2, b_gate, g_o, w_o):
    n, d, dk = w_q.shape
    low_pad = jnp.zeros((n, d, LANES - 2 * GLA_RANK), w_q.dtype)
    wg = jnp.concatenate([w_q, w_k, w_v, w_r, w_gate1[:, 0], w_gate1[:, 1], low_pad], axis=2)
    zero = jnp.zeros((n, GLA_RANK, dk), w_gate2.dtype)
    w2 = jnp.concatenate([
        jnp.concatenate([w_gate2[:, 0], zero], axis=2),
        jnp.concatenate([zero, w_gate2[:, 1]], axis=2),
        jnp.zeros((n, LANES - 2 * GLA_RANK, 2 * dk), w_gate2.dtype)], axis=1)
    bg = jnp.concatenate([b_gate[:, 0], b_gate[:, 1]], axis=1)[:, None, :]
    return (wg.astype(BF16), w2.astype(BF16), bg), g_o[:, None, :], w_o.astype(BF16)


def kernel(x, c, ctx, c_ctx, ada_w, ada_b, norm_g, mlp_w1, mlp_w2, mla_w_dq, mla_g_q, mla_w_uq, mla_w_dkv, mla_g_kv, mla_w_ukv, mla_w_o, gla_w_q, gla_w_k, gla_w_v, gla_w_r, gla_w_gate1, gla_w_gate2, gla_b_gate, gla_g_o, gla_w_o):
    b, s, d = x.shape
    n_ctx = ctx.shape[1]
    depth = ada_w.shape[0]
    assert n_ctx == TILE and s % TILE == 0 and d == GLA_HEADS * 2 * LANES

    stream = (x, ctx)
    g = _batch_group(b, MAX_BATCH_GROUP)
    rows = -(-(b + g) // 16) * 16
    cvec = jnp.concatenate([c, jnp.tile(c_ctx[None], (g, 1)), jnp.zeros((rows - b - g, d), c.dtype)], axis=0)
    mod_all = _ada(cvec, ada_w, ada_b).reshape(depth, rows, N_MOD, d)

    rcs = _rope_tables(s, n_ctx)
    rcst = rcs.T
    tq = 512 if s % 512 == 0 else TILE

    w1_all = mlp_w1.astype(BF16)
    w2_all = mlp_w2.astype(BF16)
    mla_proj_w, mla_wo = _mla_weights(mla_w_dq, mla_g_q, mla_w_uq, mla_w_dkv, mla_g_kv, mla_w_ukv, mla_w_o)
    gla_proj_w, gla_go, gla_wo = _gla_weights(gla_w_q, gla_w_k, gla_w_v, gla_w_r, gla_w_gate1, gla_w_gate2,
                                              gla_b_gate, gla_g_o, gla_w_o)

    for i in range(depth):
        j = i // 2
        mod = _LayerParam(mod_all, i)
        ng = _LayerParam(norm_g, i)
        mlp = (_LayerParam(w1_all, i), _LayerParam(w2_all, i))
        last = i == depth - 1
        if i % 2 == 0:
            qt, k, vt = _mla_proj(stream, mod, [ng] + [_LayerParam(w, j) for w in mla_proj_w], rcs, rcst)
            o = _mla_attention(qt, k, vt, s, tq)
            weights = (_LayerParam(mla_wo, j),) + mlp
            stream = (_post(stream, mod, ng, (o,), weights, gla=False, latent_only=last),)
        else:
            (xa,) = stream
            qf, kf, kfe, qb, kb, kbe, v, r, dd = _gla_proj(xa, mod, [ng] + [_LayerParam(w, j) for w in gla_proj_w])
            of, ob = _gla_scan(qf, kf, kfe, qb, kb, kbe, v, dd)
            weights = (_LayerParam(gla_go, j), _LayerParam(gla_wo, j)) + mlp
            stream = (_post(stream, mod, ng, (of, ob, r), weights, gla=True, latent_only=last),)
    return stream[0]
```
